```python
import jax, jax.numpy as jnp
from jax import lax
import numpy as np

D_MODEL = 1024
BATCH = 8
SEQ = 2048
DEPTH = 4
DEC_BATCH = 32
DEC_SEQ = 4
PAST_LEN = 16384
PAGE_SIZE = 128

N_MIXERS = 3
N_A_LAYERS = (DEPTH + 2) // 3
N_B_LAYERS = (DEPTH + 1) // 3
N_C_LAYERS = DEPTH // 3
PLE_DIM = 256
Q_BLOCK = 128
A_Q_BLOCK = 32
ROPE_THETA = 10000.0
EPS = 1e-6
NEG = -1e30
D_FF = ((8 * D_MODEL // 3 + 255) // 256) * 256

A_GROUPS = ((128, 1), (512, 4), (2048, 16))
A_N_GROUPS = len(A_GROUPS)
A_HEADS = D_MODEL // 64
A_HEAD_DIM = 64
A_SCALE = A_HEAD_DIM ** -0.5
B_HEADS = D_MODEL // 64
B_HEAD_DIM = 64
B_SCALE = B_HEAD_DIM ** -0.5
C_HEADS = D_MODEL // 64
C_NOPE = 64
C_ROPE = 32
C_V = 64
C_Q_LORA = 3 * D_MODEL // 4
C_KV_LORA = D_MODEL // 4
C_SCALE = (C_NOPE + C_ROPE) ** -0.5

kernel_name = 'hybrid_dilated_fox_mla_decoder_step'


def _rms(x, g):
    xf = x.astype(jnp.float32)
    y = xf * lax.rsqrt(jnp.mean(xf * xf, axis=-1, keepdims=True) + EPS)
    return (y * g.astype(jnp.float32)).astype(x.dtype)


def _rope(x, pos):
    dim = x.shape[-1]
    half = dim // 2
    inv = ROPE_THETA ** (-jnp.arange(half, dtype=jnp.float32) * 2.0 / dim)
    ang = pos.astype(jnp.float32)[:, None] * inv[None, :]
    shape = (1, pos.shape[0]) + (1,) * (x.ndim - 3) + (half,)
    cos = jnp.cos(ang).reshape(shape)
    sin = jnp.sin(ang).reshape(shape)
    xf = x.astype(jnp.float32)
    x1, x2 = xf[..., :half], xf[..., half:]
    return jnp.concatenate([x1 * cos - x2 * sin, x2 * cos + x1 * sin], axis=-1).astype(x.dtype)


def _swiglu(x, w_in, w_out):
    g, u = jnp.split(x @ w_in, 2, axis=-1)
    return (jax.nn.silu(g) * u) @ w_out


def _ple(x, p, g, w_gate, w_proj):
    gate = jax.nn.sigmoid(_rms(x, g) @ w_gate)
    return x + (gate * (p @ w_proj)).astype(x.dtype)


def _sweep(fn, seq_len, block):
    n = seq_len // block
    out = lax.map(fn, jnp.arange(n, dtype=jnp.int32) * block)
    out = jnp.swapaxes(out, 0, 1)
    return out.reshape((out.shape[0], n * block) + out.shape[3:])


def _online(m, l, acc, s, v, eq):
    m_new = jnp.maximum(m, s.max(-1))
    a = jnp.exp(m - m_new)
    p = jnp.exp(s - m_new[..., None])
    return m_new, l * a + p.sum(-1), acc * a[..., None] + jnp.einsum(eq, p, v.astype(jnp.float32))


def _a_project(x, w_in, pos):
    b, n, _ = x.shape
    qkv = (x @ w_in).reshape(b, n, A_N_GROUPS, 3, A_HEADS, A_HEAD_DIM)
    return _rope(qkv[:, :, :, 0], pos), _rope(qkv[:, :, :, 1], pos), qkv[:, :, :, 2]


def _dilated_attend(q, k_seq, v_seq, q_local, dilation, n_keys):
    idx = q_local[:, None] - dilation * jnp.arange(n_keys)[None, :]
    valid = idx >= 0
    idx = jnp.maximum(idx, 0)
    kg = jnp.take(k_seq, idx, axis=1)
    vg = jnp.take(v_seq, idx, axis=1)
    s = jnp.einsum('bqhd,bqkhd->bhqk', q.astype(jnp.float32), kg.astype(jnp.float32)) * A_SCALE
    s = jnp.where(valid[None, None], s, NEG)
    lse = jax.nn.logsumexp(s, axis=-1)
    p = jnp.exp(s - lse[..., None])
    return jnp.einsum('bhqk,bqkhd->bqhd', p, vg.astype(jnp.float32)), lse


def _a_combine(q, kv_seqs, q_locals):
    outs, lses = [], []
    for g, (window, dil) in enumerate(A_GROUPS):
        o, lse = _dilated_attend(q[:, :, g], kv_seqs[g][0], kv_seqs[g][1], q_locals[g], dil, window // dil + 1)
        outs.append(o)
        lses.append(lse)
    wts = jax.nn.softmax(jnp.stack(lses, 0), axis=0)
    return jnp.einsum('gbhq,gbqhd->bqhd', wts, jnp.stack(outs, 0))


def _a_prompt(x, w_in, w_out):
    b, n, _ = x.shape
    q, k, v = _a_project(x, w_in, jnp.arange(n))
    kv_seqs = [(k[:, :, g], v[:, :, g]) for g in range(A_N_GROUPS)]

    def block(q0):
        qb = lax.dynamic_slice_in_dim(q, q0, A_Q_BLOCK, axis=1)
        ql = q0 + jnp.arange(A_Q_BLOCK)
        return _a_combine(qb, kv_seqs, [ql] * A_N_GROUPS)

    o = _sweep(block, n, A_Q_BLOCK)
    y = o.reshape(b, n, -1).astype(x.dtype) @ w_out
    states = [jnp.stack([k[:, :, g], v[:, :, g]], axis=2)[:, n - min(w, n):]
              for g, (w, _) in enumerate(A_GROUPS)]
    return y, states


def _a_sample(x, bufs, w_in, w_out, past_len):
    b, n, _ = x.shape
    q, k, v = _a_project(x, w_in, past_len + jnp.arange(n))
    kv_seqs, q_locals, states = [], [], []
    for g, (w, _) in enumerate(A_GROUPS):
        buf_len = bufs[g].shape[1]
        full = jnp.concatenate([bufs[g], jnp.stack([k[:, :, g], v[:, :, g]], axis=2)], axis=1)
        kv_seqs.append((full[:, :, 0], full[:, :, 1]))
        q_locals.append(buf_len + jnp.arange(n))
        keep = min(w, buf_len + n)
        states.append(full[:, buf_len + n - keep:])
    o = _a_combine(q, kv_seqs, q_locals)
    return o.reshape(b, n, -1).astype(x.dtype) @ w_out, states


def _b_project(x, w_in, b_f):
    b, n, _ = x.shape
    inner = B_HEADS * B_HEAD_DIM
    y = x @ w_in
    qkv = y[..., :3 * inner].reshape(b, n, 3, B_HEADS, B_HEAD_DIM)
    logf = jax.nn.log_sigmoid((y[..., 3 * inner:] + b_f).astype(jnp.float32))
    return qkv[:, :, 0], qkv[:, :, 1], qkv[:, :, 2], logf


def _b_prompt(x, w_in, b_f, w_out):
    b, n, _ = x.shape
    q, k, v, logf = _b_project(x, w_in, b_f)
    c = jnp.swapaxes(jnp.cumsum(logf, axis=1), 1, 2)
    kf, vf = k.astype(jnp.float32), v.astype(jnp.float32)
    kpos = jnp.arange(n)

    def block(q0):
        qb = lax.dynamic_slice_in_dim(q, q0, Q_BLOCK, axis=1).astype(jnp.float32)
        cb = lax.dynamic_slice_in_dim(c, q0, Q_BLOCK, axis=2)
        s = jnp.einsum('bqhd,bkhd->bhqk', qb, kf) * B_SCALE + (cb[..., :, None] - c[:, :, None, :])
        qpos = q0 + jnp.arange(Q_BLOCK)
        s = jnp.where(kpos[None, :] <= qpos[:, None], s, NEG)
        p = jax.nn.softmax(s, axis=-1)
        return jnp.einsum('bhqk,bkhd->bqhd', p, vf)

    o = _sweep(block, n, Q_BLOCK)
    y = o.reshape(b, n, -1).astype(x.dtype) @ w_out
    return y, jnp.stack([k, v], axis=2), logf


def _b_sample(x, cache_kv, cache_logf, j, page_table, w_in, b_f, w_out):
    b, n, _ = x.shape
    q, k, v, logf = _b_project(x, w_in, b_f)
    qf = q.astype(jnp.float32)
    a_t = jnp.swapaxes(jnp.cumsum(logf, axis=1), 1, 2)
    s_new = jnp.einsum('bqhd,bkhd->bhqk', qf, k.astype(jnp.float32)) * B_SCALE + (a_t[..., :, None] - a_t[..., None, :])
    s_new = jnp.where(jnp.tril(jnp.ones((n, n), dtype=bool)), s_new, NEG)
    eq = 'bhqk,bkhd->bhqd'
    m0 = jnp.full((b, B_HEADS, n), NEG, jnp.float32)
    l0 = jnp.zeros((b, B_HEADS, n), jnp.float32)
    acc0 = jnp.zeros((b, B_HEADS, n, B_HEAD_DIM), jnp.float32)
    m, l, acc = _online(m0, l0, acc0, s_new, v, eq)

    def step(carry, phys):
        m, l, acc, tail = carry
        kv = cache_kv[j, phys]
        lf = cache_logf[j, phys].astype(jnp.float32)
        r = lax.cumsum(lf, axis=1, reverse=True) - lf + tail[:, None, :]
        s = (jnp.einsum('bqhd,bkhd->bhqk', qf, kv[:, :, 0].astype(jnp.float32)) * B_SCALE
             + a_t[..., None] + jnp.swapaxes(r, 1, 2)[:, :, None, :])
        m, l, acc = _online(m, l, acc, s, kv[:, :, 1], eq)
        return (m, l, acc, tail + lf.sum(1)), None

    init = (m, l, acc, jnp.zeros((b, B_HEADS), jnp.float32))
    (m, l, acc, _), _ = lax.scan(step, init, page_table.T, reverse=True)
    o = jnp.swapaxes(acc / l[..., None], 1, 2)
    y = o.reshape(b, n, -1).astype(x.dtype) @ w_out
    return y, jnp.stack([k, v], axis=2), logf


def _c_project(x, w_in, q_norm, kv_norm, w_qb, w_kvb, pos):
    b, n, _ = x.shape
    y = x @ w_in
    cq = _rms(y[..., :C_Q_LORA], q_norm)
    ckv = _rms(y[..., C_Q_LORA:C_Q_LORA + C_KV_LORA], kv_norm)
    kpe = _rope(y[..., C_Q_LORA + C_KV_LORA:], pos)
    qh = (cq @ w_qb).reshape(b, n, C_HEADS, C_NOPE + C_ROPE)
    q_pe = _rope(qh[..., C_NOPE:], pos)
    w_uk = w_kvb.reshape(C_KV_LORA, C_HEADS, C_NOPE + C_V)[..., :C_NOPE]
    q_lat = jnp.einsum('bnhc,lhc->bnhl', qh[..., :C_NOPE], w_uk)
    return q_lat, q_pe, ckv, kpe


def _c_up(o_lat, w_kvb, w_out, dtype):
    b, n = o_lat.shape[:2]
    w_uv = w_kvb.reshape(C_KV_LORA, C_HEADS, C_NOPE + C_V)[..., C_NOPE:]
    o = jnp.einsum('bnhl,lhv->bnhv', o_lat, w_uv.astype(jnp.float32))
    return o.reshape(b, n, -1).astype(dtype) @ w_out


def _c_prompt(x, w_in, q_norm, kv_norm, w_qb, w_kvb, w_out):
    b, n, _ = x.shape
    q_lat, q_pe, ckv, kpe = _c_project(x, w_in, q_norm, kv_norm, w_qb, w_kvb, jnp.arange(n))
    cf, kpf = ckv.astype(jnp.float32), kpe.astype(jnp.float32)
    kpos = jnp.arange(n)

    def block(q0):
        ql = lax.dynamic_slice_in_dim(q_lat, q0, Q_BLOCK, axis=1).astype(jnp.float32)
        qp = lax.dynamic_slice_in_dim(q_pe, q0, Q_BLOCK, axis=1).astype(jnp.float32)
        s = (jnp.einsum('bqhl,bkl->bhqk', ql, cf) + jnp.einsum('bqhr,bkr->bhqk', qp, kpf)) * C_SCALE
        qpos = q0 + jnp.arange(Q_BLOCK)
        s = jnp.where(kpos[None, :] <= qpos[:, None], s, NEG)
        p = jax.nn.softmax(s, axis=-1)
        return jnp.einsum('bhqk,bkl->bqhl', p, cf)

    o_lat = _sweep(block, n, Q_BLOCK)
    return _c_up(o_lat, w_kvb, w_out, x.dtype), jnp.concatenate([ckv, kpe], axis=-1)


def _c_sample(x, cache, j, page_table, w_in, q_norm, kv_norm, w_qb, w_kvb, w_out, past_len):
    b, n, _ = x.shape
    q_lat, q_pe, ckv, kpe = _c_project(x, w_in, q_norm, kv_norm, w_qb, w_kvb, past_len + jnp.arange(n))
    qlf, qpf = q_lat.astype(jnp.float32), q_pe.astype(jnp.float32)

    def scores(c, kp):
        return (jnp.einsum('bqhl,bkl->bhqk', qlf, c.astype(jnp.float32))
                + jnp.einsum('bqhr,bkr->bhqk', qpf, kp.astype(jnp.float32))) * C_SCALE

    eq = 'bhqk,bkl->bhql'
    s_new = jnp.where(jnp.tril(jnp.ones((n, n), dtype=bool)), scores(ckv, kpe), NEG)
    m0 = jnp.full((b, C_HEADS, n), NEG, jnp.float32)
    l0 = jnp.zeros((b, C_HEADS, n), jnp.float32)
    acc0 = jnp.zeros((b, C_HEADS, n, C_KV_LORA), jnp.float32)
    carry = _online(m0, l0, acc0, s_new, ckv, eq)

    def step(carry, phys):
        rows = cache[j, phys]
        c = rows[..., :C_KV_LORA]
        return _online(*carry, scores(c, rows[..., C_KV_LORA:]), c, eq), None

    (m, l, acc), _ = lax.scan(step, carry, page_table.T)
    o_lat = jnp.swapaxes(acc / l[..., None], 1, 2)
    return _c_up(o_lat, w_kvb, w_out, x.dtype), jnp.concatenate([ckv, kpe], axis=-1)


def setup_inputs(seed: int = 0) -> dict:
    key = jax.random.key(seed)
    keys = iter(jax.random.split(key, 40))
    f32 = jnp.float32

    def rnd(shape, scale=1.0):
        return jax.random.normal(next(keys), shape, f32) * scale

    def gain(shape):
        return 1.0 + 0.05 * rnd(shape)

    n_pages = PAST_LEN // PAGE_SIZE
    n_pool = (5 * DEC_BATCH * n_pages) // 4
    page_table = jax.random.permutation(next(keys), n_pool)[:DEC_BATCH * n_pages].reshape(DEC_BATCH, n_pages).astype(jnp.int32)
    a_inner = A_HEADS * A_HEAD_DIM
    b_inner = B_HEADS * B_HEAD_DIM
    c_inner = C_HEADS * C_V
    return {
        'x_prompt': rnd((BATCH, SEQ, D_MODEL)),
        'x_sample': rnd((DEC_BATCH, DEC_SEQ, D_MODEL)),
        'cache_a_kv0': rnd((N_A_LAYERS, DEC_BATCH, min(A_GROUPS[0][0], PAST_LEN), 2, A_HEADS, A_HEAD_DIM)),
        'cache_a_kv1': rnd((N_A_LAYERS, DEC_BATCH, min(A_GROUPS[1][0], PAST_LEN), 2, A_HEADS, A_HEAD_DIM)),
        'cache_a_kv2': rnd((N_A_LAYERS, DEC_BATCH, min(A_GROUPS[2][0], PAST_LEN), 2, A_HEADS, A_HEAD_DIM)),
        'cache_b_kv': rnd((N_B_LAYERS, n_pool, PAGE_SIZE, 2, B_HEADS, B_HEAD_DIM)),
        'cache_b_logf': jax.nn.log_sigmoid(3.0 + rnd((N_B_LAYERS, n_pool, PAGE_SIZE, B_HEADS))),
        'cache_c_ckv': rnd((N_C_LAYERS, n_pool, PAGE_SIZE, C_KV_LORA + C_ROPE)),
        'page_table': page_table,
        'p_prompt': rnd((DEPTH, BATCH, SEQ, PLE_DIM)),
        'p_sample': rnd((DEPTH, DEC_BATCH, DEC_SEQ, PLE_DIM)),
        'norm_g': gain((DEPTH, 4, D_MODEL)),
        'final_g': gain((D_MODEL,)),
        'ffn_w_in': rnd((DEPTH, 2, D_MODEL, 2 * D_FF), D_MODEL ** -0.5),
        'ffn_w_out': rnd((DEPTH, 2, D_FF, D_MODEL), D_FF ** -0.5),
        'ple_w_gate': rnd((DEPTH, D_MODEL, D_MODEL), D_MODEL ** -0.5),
        'ple_w_proj': rnd((DEPTH, PLE_DIM, D_MODEL), PLE_DIM ** -0.5),
        'a_w_in': rnd((N_A_LAYERS, D_MODEL, A_N_GROUPS * 3 * a_inner), D_MODEL ** -0.5),
        'a_w_out': rnd((N_A_LAYERS, a_inner, D_MODEL), a_inner ** -0.5),
        'b_w_in': rnd((N_B_LAYERS, D_MODEL, 3 * b_inner + B_HEADS), D_MODEL ** -0.5),
        'b_f_bias': jax.random.uniform(next(keys), (N_B_LAYERS, B_HEADS), f32, 1.0, 6.0),
        'b_w_out': rnd((N_B_LAYERS, b_inner, D_MODEL), b_inner ** -0.5),
        'c_w_in': rnd((N_C_LAYERS, D_MODEL, C_Q_LORA + C_KV_LORA + C_ROPE), D_MODEL ** -0.5),
        'c_q_norm': gain((N_C_LAYERS, C_Q_LORA)),
        'c_kv_norm': gain((N_C_LAYERS, C_KV_LORA)),
        'c_w_qb': rnd((N_C_LAYERS, C_Q_LORA, C_HEADS * (C_NOPE + C_ROPE)), C_Q_LORA ** -0.5),
        'c_w_kvb': rnd((N_C_LAYERS, C_KV_LORA, C_HEADS * (C_NOPE + C_V)), C_KV_LORA ** -0.5),
        'c_w_out': rnd((N_C_LAYERS, c_inner, D_MODEL), c_inner ** -0.5),
    }


def reference(x_prompt, x_sample, cache_a_kv0, cache_a_kv1, cache_a_kv2, cache_b_kv, cache_b_logf,
              cache_c_ckv, page_table, p_prompt, p_sample, norm_g, final_g, ffn_w_in, ffn_w_out,
              ple_w_gate, ple_w_proj, a_w_in, a_w_out, b_w_in, b_f_bias, b_w_out, c_w_in, c_q_norm,
              c_kv_norm, c_w_qb, c_w_kvb, c_w_out):
    past_len = page_table.shape[1] * cache_b_kv.shape[2]
    a_bufs = (cache_a_kv0, cache_a_kv1, cache_a_kv2)
    xp, xs = x_prompt, x_sample
    a_new_p = [[] for _ in A_GROUPS]
    a_new_s = [[] for _ in A_GROUPS]
    b_kv_p, b_lf_p, b_kv_s, b_lf_s, c_p, c_s = [], [], [], [], [], []
    for i in range(DEPTH):
        kind, j = i % N_MIXERS, i // N_MIXERS
        xp = xp + 0.5 * _swiglu(_rms(xp, norm_g[i, 0]), ffn_w_in[i, 0], ffn_w_out[i, 0])
        xs = xs + 0.5 * _swiglu(_rms(xs, norm_g[i, 0]), ffn_w_in[i, 0], ffn_w_out[i, 0])
        hp, hs = _rms(xp, norm_g[i, 1]), _rms(xs, norm_g[i, 1])
        if kind == 0:
            yp, st_p = _a_prompt(hp, a_w_in[j], a_w_out[j])
            ys, st_s = _a_sample(hs, [buf[j] for buf in a_bufs], a_w_in[j], a_w_out[j], past_len)
            for g in range(A_N_GROUPS):
                a_new_p[g].append(st_p[g])
                a_new_s[g].append(st_s[g])
        elif kind == 1:
            yp, kvp, lfp = _b_prompt(hp, b_w_in[j], b_f_bias[j], b_w_out[j])
            ys, kvs, lfs = _b_sample(hs, cache_b_kv, cache_b_logf, j, page_table, b_w_in[j], b_f_bias[j], b_w_out[j])
            b_kv_p.append(kvp)
            b_lf_p.append(lfp)
            b_kv_s.append(kvs)
            b_lf_s.append(lfs)
        else:
            yp, cp = _c_prompt(hp, c_w_in[j], c_q_norm[j], c_kv_norm[j], c_w_qb[j], c_w_kvb[j], c_w_out[j])
            ys, cs = _c_sample(hs, cache_c_ckv, j, page_table, c_w_in[j], c_q_norm[j], c_kv_norm[j],
                               c_w_qb[j], c_w_kvb[j], c_w_out[j], past_len)
            c_p.append(cp)
            c_s.append(cs)
        xp = xp + yp.astype(xp.dtype)
        xs = xs + ys.astype(xs.dtype)
        xp = xp + 0.5 * _swiglu(_rms(xp, norm_g[i, 2]), ffn_w_in[i, 1], ffn_w_out[i, 1])
        xs = xs + 0.5 * _swiglu(_rms(xs, norm_g[i, 2]), ffn_w_in[i, 1], ffn_w_out[i, 1])
        xp = _ple(xp, p_prompt[i], norm_g[i, 3], ple_w_gate[i], ple_w_proj[i])
        xs = _ple(xs, p_sample[i], norm_g[i, 3], ple_w_gate[i], ple_w_proj[i])
    return (_rms(xp, final_g), _rms(xs, final_g),
            jnp.stack(a_new_p[0]), jnp.stack(a_new_p[1]), jnp.stack(a_new_p[2]),
            jnp.stack(b_kv_p), jnp.stack(b_lf_p), jnp.stack(c_p),
            jnp.stack(a_new_s[0]), jnp.stack(a_new_s[1]), jnp.stack(a_new_s[2]),
            jnp.stack(b_kv_s), jnp.stack(b_lf_s), jnp.stack(c_s))
```

```python
import functools

import jax
import jax.numpy as jnp
from jax import lax
from jax.experimental import pallas as pl
from jax.experimental.pallas import tpu as pltpu

F32 = jnp.float32
BF16 = jnp.bfloat16

EPS = 1e-6
NEG = -1e30
ROPE_THETA = 10000.0
N_MIXERS = 3
A_GROUPS = ((128, 1), (512, 4), (2048, 16))
HEAD_DIM = 64
C_NOPE, C_ROPE, C_V = 64, 32, 64

LANES = 128
VMEM_LIMIT_BYTES = 56 * 1024 * 1024


def _params(*semantics):
    return pltpu.CompilerParams(dimension_semantics=semantics,
                                vmem_limit_bytes=VMEM_LIMIT_BYTES)


def _tile(m, candidates):
    for c in candidates:
        if m % c == 0:
            return c
    return m


def _rms(x, g):
    ms = jnp.mean(x * x, axis=-1, keepdims=True)
    return x * lax.rsqrt(ms + EPS) * g


def _dot(a, b):
    return jnp.dot(a, b, preferred_element_type=F32)


def _dot_nt(a, b):
    return lax.dot_general(a, b, (((1,), (1,)), ((), ())), preferred_element_type=F32)


def _swap_halves(y, half):
    width = y.shape[-1]
    lane = lax.broadcasted_iota(jnp.int32, y.shape, y.ndim - 1)
    first = (lane % (2 * half)) < half
    return jnp.where(first, pltpu.roll(y, width - half, y.ndim - 1), pltpu.roll(y, half, y.ndim - 1))


def _rope_tables(pos, dim, reps, pad_to=None):
    half = dim // 2
    inv = ROPE_THETA ** (-jnp.arange(half, dtype=F32) * 2.0 / dim)
    ang = pos.astype(F32)[:, None] * inv[None, :]
    cos, sin = jnp.cos(ang), jnp.sin(ang)
    c = jnp.concatenate([cos, cos], axis=-1)
    s = jnp.concatenate([-sin, sin], axis=-1)
    if pad_to is not None:
        z = jnp.zeros((pos.shape[0], pad_to - dim), F32)
        c = jnp.concatenate([c, z], axis=-1)
        s = jnp.concatenate([s, z], axis=-1)
    return jnp.tile(c, (1, reps)), jnp.tile(s, (1, reps))


def _ffn_kernel(x_ref, g_ref, wg_ref, wu_ref, wo_ref, o_ref, h_ref, acc_ref):
    j = pl.program_id(1)

    @pl.when(j == 0)
    def _():
        h_ref[...] = _rms(x_ref[...], g_ref[...]).astype(BF16)
        acc_ref[...] = jnp.zeros_like(acc_ref)

    h = h_ref[...]
    gate = _dot(h, wg_ref[...])
    up = _dot(h, wu_ref[...])
    act = (gate * jax.nn.sigmoid(gate) * up).astype(BF16)
    acc_ref[...] += _dot(act, wo_ref[...])

    @pl.when(j == pl.num_programs(1) - 1)
    def _():
        o_ref[...] = x_ref[...] + 0.5 * acc_ref[...]


def _ffn(x, g, w_in, w_out):
    m, d = x.shape
    f = w_out.shape[0]
    tm = _tile(m, (1024, 512, 256, 128, 64, 32, 16, 8))
    tf = _tile(f, (256, 128))
    nf = f // tf
    return pl.pallas_call(
        _ffn_kernel,
        grid=(m // tm, nf),
        in_specs=[
            pl.BlockSpec((tm, d), lambda i, j: (i, 0)),
            pl.BlockSpec((1, d), lambda i, j: (0, 0)),
            pl.BlockSpec((d, tf), lambda i, j: (0, j)),
            pl.BlockSpec((d, tf), lambda i, j: (0, j + nf)),
            pl.BlockSpec((tf, d), lambda i, j: (j, 0)),
        ],
        out_specs=pl.BlockSpec((tm, d), lambda i, j: (i, 0)),
        out_shape=jax.ShapeDtypeStruct((m, d), F32),
        scratch_shapes=[pltpu.VMEM((tm, d), BF16), pltpu.VMEM((tm, d), F32)],
        compiler_params=_params("parallel", "arbitrary"),
        name="ffn_half",
    )(x, g.reshape(1, d), w_in, w_in, w_out)


def _ple_kernel(x_ref, p_ref, g_ref, wg_ref, wp_ref, o_ref):
    x = x_ref[...]
    gate = jax.nn.sigmoid(_dot(_rms(x, g_ref[...]).astype(BF16), wg_ref[...]))
    o_ref[...] = x + gate * _dot(p_ref[...].astype(BF16), wp_ref[...])


def _ple_final_kernel(x_ref, p_ref, g_ref, wg_ref, wp_ref, fg_ref, o_ref):
    x = x_ref[...]
    gate = jax.nn.sigmoid(_dot(_rms(x, g_ref[...]).astype(BF16), wg_ref[...]))
    y = x + gate * _dot(p_ref[...].astype(BF16), wp_ref[...])
    o_ref[...] = _rms(y, fg_ref[...])


def _ple(x, p, g, w_gate, w_proj, final_g=None):
    m, d = x.shape
    pd = p.shape[1]
    tm = _tile(m, (512, 256, 128, 64, 32, 16, 8))
    row = lambda i: (i, 0)
    fix = lambda i: (0, 0)
    in_specs = [
        pl.BlockSpec((tm, d), row),
        pl.BlockSpec((tm, pd), row),
        pl.BlockSpec((1, d), fix),
        pl.BlockSpec((d, d), fix),
        pl.BlockSpec((pd, d), fix),
    ]
    args = [x, p, g.reshape(1, d), w_gate, w_proj]
    body = _ple_kernel
    if final_g is not None:
        in_specs.append(pl.BlockSpec((1, d), fix))
        args.append(final_g.reshape(1, d))
        body = _ple_final_kernel
    return pl.pallas_call(
        body,
        grid=(m // tm,),
        in_specs=in_specs,
        out_specs=pl.BlockSpec((tm, d), row),
        out_shape=jax.ShapeDtypeStruct((m, d), F32),
        compiler_params=_params("parallel"),
        name="ple",
    )(*args)


def _outproj_kernel(x_ref, o_ref_in, w_ref, y_ref):
    y_ref[...] = x_ref[...] + _dot(o_ref_in[...].astype(BF16), w_ref[...])


def _outproj(x, o, w):
    m, d = x.shape
    k = o.shape[1]
    tm = _tile(m, (512, 256, 128, 64, 32, 16, 8))
    return pl.pallas_call(
        _outproj_kernel,
        grid=(m // tm,),
        in_specs=[
            pl.BlockSpec((tm, d), lambda i: (i, 0)),
            pl.BlockSpec((tm, k), lambda i: (i, 0)),
            pl.BlockSpec((k, d), lambda i: (0, 0)),
        ],
        out_specs=pl.BlockSpec((tm, d), lambda i: (i, 0)),
        out_shape=jax.ShapeDtypeStruct((m, d), F32),
        compiler_params=_params("parallel"),
        name="outproj",
    )(x, o, w)


def _a_project_kernel(scale, x_ref, g_ref, w_ref, cos_ref, sin_ref, o_ref, h_ref):
    j = pl.program_id(1)

    @pl.when(j == 0)
    def _():
        h_ref[...] = _rms(x_ref[...], g_ref[...]).astype(BF16)

    y = _dot(h_ref[...], w_ref[...])
    kind = j % 3
    rot = y * cos_ref[...] + _swap_halves(y, HEAD_DIM // 2) * sin_ref[...]
    rot = rot * jnp.where(kind == 0, scale, 1.0)
    o_ref[...] = jnp.where(kind < 2, rot, y)


def _a_project(x, g, w_in, cos_t, sin_t):
    m, d = x.shape
    n_out = w_in.shape[1]
    inner = cos_t.shape[1]
    p = cos_t.shape[0]
    tm = _tile(p, (512, 256, 128, 64, 32, 16, 8))
    npb = p // tm
    return pl.pallas_call(
        functools.partial(_a_project_kernel, HEAD_DIM ** -0.5),
        grid=(m // tm, n_out // inner),
        in_specs=[
            pl.BlockSpec((tm, d), lambda i, j: (i, 0)),
            pl.BlockSpec((1, d), lambda i, j: (0, 0)),
            pl.BlockSpec((d, inner), lambda i, j: (0, j)),
            pl.BlockSpec((tm, inner), lambda i, j: (i % npb, 0)),
            pl.BlockSpec((tm, inner), lambda i, j: (i % npb, 0)),
        ],
        out_specs=pl.BlockSpec((tm, inner), lambda i, j: (i, j)),
        out_shape=jax.ShapeDtypeStruct((m, n_out), F32),
        scratch_shapes=[pltpu.VMEM((tm, d), BF16)],
        compiler_params=_params("parallel", "arbitrary"),
        name="a_project",
    )(x, g.reshape(1, d), w_in, cos_t, sin_t)


def _band_valid(tq, nk, offset, n_back):
    row = lax.broadcasted_iota(jnp.int32, (tq, nk), 0)
    col = lax.broadcasted_iota(jnp.int32, (tq, nk), 1)
    diff = row + offset - col
    return lax.bitcast_convert_type(diff, jnp.uint32) <= jnp.uint32(n_back)


def _a_prompt_kernel(n_back, q_ref, k_ref, v_ref, o_ref, lse_ref):
    length = q_ref.shape[1]
    tq = min(128, length)
    lane = lax.broadcasted_iota(jnp.int32, (tq, LANES), 1)
    low = lane < HEAD_DIM
    for i in range(length // tq):
        q0 = i * tq
        k0 = max(0, q0 - n_back)
        nk = q0 + tq - k0
        valid = _band_valid(tq, nk, q0 - k0, n_back)
        q = q_ref[0, q0:q0 + tq, :]
        kseg = k_ref[0, k0:k0 + nk, :].astype(BF16)
        vseg = v_ref[0, k0:k0 + nk, :].astype(BF16)
        outs, lses = [], []
        for head_low in (True, False):
            sel = low if head_low else jnp.logical_not(low)
            s = _dot_nt(jnp.where(sel, q, 0.0).astype(BF16), kseg)
            s = jnp.where(valid, s, NEG)
            mx = jnp.max(s, axis=-1, keepdims=True)
            p = jnp.exp(s - mx)
            den = jnp.sum(p, axis=-1, keepdims=True)
            outs.append(_dot(p.astype(BF16), vseg) / den)
            lses.append(mx + jnp.log(den))
        o_ref[0, q0:q0 + tq, :] = jnp.where(low, outs[0], outs[1])
        lse_ref[0, q0:q0 + tq, :] = jnp.where(low, lses[0], lses[1])


def _a_prompt_group(qkv, batch, n, g, n_groups, inner):
    window, dil = A_GROUPS[g]
    n_back = window // dil
    length = n // dil
    width = n_groups * 3 * inner
    nhp = inner // LANES
    view = qkv.reshape(batch, length, dil * width)

    def spec(kind):
        return pl.BlockSpec((1, length, LANES),
                            lambda b, r, hp: (b, 0, (r * n_groups * 3 + g * 3 + kind) * nhp + hp))

    out_spec = pl.BlockSpec((1, length, LANES), lambda b, r, hp: (b, 0, r * nhp + hp))
    shape = jax.ShapeDtypeStruct((batch, length, dil * inner), F32)
    o, lse = pl.pallas_call(
        functools.partial(_a_prompt_kernel, n_back),
        grid=(batch, dil, nhp),
        in_specs=[spec(0), spec(1), spec(2)],
        out_specs=[out_spec, out_spec],
        out_shape=[shape, shape],
        compiler_params=_params("parallel", "parallel", "parallel"),
        name=f"a_prompt_g{g}",
    )(view, view, view)
    return o.reshape(batch * n, inner), lse.reshape(batch * n, inner)


def _a_sample_kernel(n_q, n_heads, dil, n_back, classes, *refs):
    q_ref, kn_ref, vn_ref = refs[:3]
    cls_refs = refs[3:3 + 2 * len(classes)]
    o_ref, lse_ref, kn_scr, vn_scr = refs[3 + 2 * len(classes):]
    inner = q_ref.shape[2]
    length = cls_refs[0].shape[1]
    buf_len = length * dil
    rows = n_q * n_heads

    kn_scr[...] = jnp.zeros_like(kn_scr)
    vn_scr[...] = jnp.zeros_like(vn_scr)
    kn_scr[0:n_q, :] = kn_ref[0]
    vn_scr[0:n_q, :] = vn_ref[0]

    lane_head = lax.broadcasted_iota(jnp.int32, (n_heads, inner), 1) // HEAD_DIM
    own = lane_head == lax.broadcasted_iota(jnp.int32, (n_heads, inner), 0)
    q = q_ref[0]
    qbd = jnp.concatenate(
        [jnp.where(own, jnp.broadcast_to(q[i:i + 1, :], (n_heads, inner)), 0.0) for i in range(n_q)],
        axis=0).astype(BF16)

    def mask(s, key_pos):
        qpos = buf_len + lax.broadcasted_iota(jnp.int32, s.shape, 0) // n_heads
        diff = qpos - key_pos
        s = jnp.where((diff & (dil - 1)) == 0, s, NEG)
        return jnp.where(lax.bitcast_convert_type(diff, jnp.uint32) <= jnp.uint32(n_back * dil), s, NEG)

    pieces = []
    for idx, c in enumerate(classes):
        col = lax.broadcasted_iota(jnp.int32, (rows, length), 1)
        pieces.append((cls_refs[2 * idx][0].astype(BF16), cls_refs[2 * idx + 1][0].astype(BF16),
                       col * dil + c))
    n_pad = kn_scr.shape[0]
    col = lax.broadcasted_iota(jnp.int32, (rows, n_pad), 1)
    new_pos = jnp.where(col < n_q, buf_len + col, buf_len + n_q)
    pieces.append((kn_scr[...].astype(BF16), vn_scr[...].astype(BF16), new_pos))

    scores = [mask(_dot_nt(qbd, kk), key_pos) for kk, _, key_pos in pieces]
    mx = functools.reduce(jnp.maximum, [jnp.max(s, axis=-1, keepdims=True) for s in scores])
    probs = [jnp.exp(s - mx) for s in scores]
    den = functools.reduce(lambda a, b: a + b, [jnp.sum(p, axis=-1, keepdims=True) for p in probs])
    acc = functools.reduce(lambda a, b: a + b,
                           [_dot(p.astype(BF16), vv) for p, (_, vv, _) in zip(probs, pieces)])
    out = acc / den
    lse = jnp.broadcast_to(mx + jnp.log(den), (rows, inner))
    for i in range(n_q):
        sl = slice(i * n_heads, (i + 1) * n_heads)
        o_ref[0, i:i + 1, :] = jnp.sum(jnp.where(own, out[sl], 0.0), axis=0, keepdims=True)
        lse_ref[0, i:i + 1, :] = jnp.sum(jnp.where(own, lse[sl], 0.0), axis=0, keepdims=True)


def _a_sample_group(qkv, buf, batch, n_q, g, n_groups, inner):
    window, dil = A_GROUPS[g]
    n_back = window // dil
    buf_len = buf.shape[1]
    assert buf_len % dil == 0 and dil & (dil - 1) == 0
    length = buf_len // dil
    n_heads = inner // HEAD_DIM
    classes = sorted({(buf_len + i) % dil for i in range(n_q)})
    qv = qkv.reshape(batch, n_q, n_groups * 3 * inner)
    bv = buf.reshape(batch, length, dil * 2 * inner)

    def qspec(kind):
        return pl.BlockSpec((1, n_q, inner), lambda b: (b, 0, g * 3 + kind))

    in_specs = [qspec(0), qspec(1), qspec(2)]
    args = [qv, qv, qv]
    for c in classes:
        for kind in (0, 1):
            in_specs.append(pl.BlockSpec((1, length, inner), lambda b, c=c, kind=kind: (b, 0, 2 * c + kind)))
            args.append(bv)
    out_spec = pl.BlockSpec((1, n_q, inner), lambda b: (b, 0, 0))
    shape = jax.ShapeDtypeStruct((batch, n_q, inner), F32)
    o, lse = pl.pallas_call(
        functools.partial(_a_sample_kernel, n_q, n_heads, dil, n_back, tuple(classes)),
        grid=(batch,),
        in_specs=in_specs,
        out_specs=[out_spec, out_spec],
        out_shape=[shape, shape],
        scratch_shapes=[pltpu.VMEM((LANES, inner), F32), pltpu.VMEM((LANES, inner), F32)],
        compiler_params=_params("parallel"),
        name=f"a_sample_g{g}",
    )(*args)
    return o.reshape(batch * n_q, inner), lse.reshape(batch * n_q, inner)


def _a_combine_kernel(n_groups, *refs):
    x_ref = refs[0]
    o_refs = refs[1:1 + n_groups]
    l_refs = refs[1 + n_groups:1 + 2 * n_groups]
    w_ref, y_ref = refs[1 + 2 * n_groups:]
    lses = [r[...] for r in l_refs]
    mx = functools.reduce(jnp.maximum, lses)
    es = [jnp.exp(l - mx) for l in lses]
    den = functools.reduce(lambda a, b: a + b, es)
    num = functools.reduce(lambda a, b: a + b, [e * r[...] for e, r in zip(es, o_refs)])
    y_ref[...] = x_ref[...] + _dot((num / den).astype(BF16), w_ref[...])


def _a_combine(x, outs, lses, w_out):
    m, d = x.shape
    inner = w_out.shape[0]
    n_groups = len(outs)
    tm = _tile(m, (256, 128, 64, 32, 16, 8))
    row = lambda i: (i, 0)
    return pl.pallas_call(
        functools.partial(_a_combine_kernel, n_groups),
        grid=(m // tm,),
        in_specs=[pl.BlockSpec((tm, d), row)] + [pl.BlockSpec((tm, inner), row)] * (2 * n_groups)
        + [pl.BlockSpec((inner, d), lambda i: (0, 0))],
        out_specs=pl.BlockSpec((tm, d), row),
        out_shape=jax.ShapeDtypeStruct((m, d), F32),
        compiler_params=_params("parallel"),
        name="a_combine",
    )(x, *outs, *lses, w_out)


def _mixer_a(xp, xs, g, w_in, w_out, bufs, batch, n, dec_batch, n_q, past_len):
    n_groups = len(A_GROUPS)
    inner = w_out.shape[0]
    n_heads = inner // HEAD_DIM
    cos_p, sin_p = _rope_tables(jnp.arange(n), HEAD_DIM, n_heads)
    cos_s, sin_s = _rope_tables(past_len + jnp.arange(n_q), HEAD_DIM, n_heads)
    cos_s, sin_s = jnp.tile(cos_s, (dec_batch, 1)), jnp.tile(sin_s, (dec_batch, 1))
    qkv_p = _a_project(xp, g, w_in, cos_p, sin_p)
    qkv_s = _a_project(xs, g, w_in, cos_s, sin_s)

    res_p = [_a_prompt_group(qkv_p, batch, n, gi, n_groups, inner) for gi in range(n_groups)]
    res_s = [_a_sample_group(qkv_s, bufs[gi], dec_batch, n_q, gi, n_groups, inner) for gi in range(n_groups)]
    xp = _a_combine(xp, [r[0] for r in res_p], [r[1] for r in res_p], w_out)
    xs = _a_combine(xs, [r[0] for r in res_s], [r[1] for r in res_s], w_out)

    qp3 = qkv_p.reshape(batch, n, n_groups, 3, inner)
    qs3 = qkv_s.reshape(dec_batch, n_q, n_groups, 3, inner)
    st_p, st_s = [], []
    for gi, (window, _) in enumerate(A_GROUPS):
        keep = min(window, n)
        st_p.append(qp3[:, n - keep:, gi, 1:3].reshape(batch, keep, 2, n_heads, HEAD_DIM))
        buf = bufs[gi]
        buf_len = buf.shape[1]
        new = qs3[:, :, gi, 1:3].reshape(dec_batch, n_q, 2, n_heads, HEAD_DIM)
        keep = min(window, buf_len + n_q)
        st_s.append(jnp.concatenate([buf, new], axis=1)[:, buf_len + n_q - keep:])
    return xp, xs, st_p, st_s


def _rms_matmul_kernel(x_ref, g_ref, w_ref, o_ref, h_ref):
    @pl.when(pl.program_id(1) == 0)
    def _():
        h_ref[...] = _rms(x_ref[...], g_ref[...]).astype(BF16)

    o_ref[...] = _dot(h_ref[...], w_ref[...])


def _rms_matmul(x, g, w):
    m, d = x.shape
    n_out = w.shape[1]
    tm = _tile(m, (512, 256, 128, 64, 32, 16, 8))
    tn = _tile(n_out, (640, 512, 256, 128))
    return pl.pallas_call(
        _rms_matmul_kernel,
        grid=(m // tm, n_out // tn),
        in_specs=[
            pl.BlockSpec((tm, d), lambda i, j: (i, 0)),
            pl.BlockSpec((1, d), lambda i, j: (0, 0)),
            pl.BlockSpec((d, tn), lambda i, j: (0, j)),
        ],
        out_specs=pl.BlockSpec((tm, tn), lambda i, j: (i, j)),
        out_shape=jax.ShapeDtypeStruct((m, n_out), F32),
        scratch_shapes=[pltpu.VMEM((tm, d), BF16)],
        compiler_params=_params("parallel", "arbitrary"),
        name="rms_matmul",
    )(x, g.reshape(1, d), w)


def _flash_kernel(n_stack, dk, dv, v_from_k, *refs):
    if v_from_k:
        q_ref, k_ref, o_ref, m_scr, l_scr, acc_scr = refs
        v_ref = k_ref
    else:
        q_ref, k_ref, v_ref, o_ref, m_scr, l_scr, acc_scr = refs
    tq = q_ref.shape[1]
    tk = tq
    rows = n_stack * tq
    i = pl.program_id(2)
    qb = q_ref[0]
    q = qb if n_stack == 1 else jnp.concatenate([qb[:, s * dk:(s + 1) * dk] for s in range(n_stack)], axis=0)
    m_scr[...] = jnp.full(m_scr.shape, NEG, F32)
    l_scr[...] = jnp.zeros(l_scr.shape, F32)
    acc_scr[...] = jnp.zeros(acc_scr.shape, F32)
    qpos = i * tq + lax.broadcasted_iota(jnp.int32, (rows, tk), 0) % tq
    col = lax.broadcasted_iota(jnp.int32, (rows, tk), 1)

    def body(kb, carry):
        k0 = pl.multiple_of(kb * tk, tk)
        kblk = k_ref[0, pl.ds(k0, tk), :]
        vblk = kblk[:, :dv] if v_from_k else v_ref[0, pl.ds(k0, tk), :]
        s = jnp.where(col + k0 <= qpos, _dot_nt(q, kblk), NEG)
        m_old = m_scr[...]
        m_new = jnp.maximum(m_old, jnp.max(s, axis=-1, keepdims=True))
        alpha = jnp.exp(m_old - m_new)
        p = jnp.exp(s - m_new)
        l_scr[...] = alpha * l_scr[...] + jnp.sum(p, axis=-1, keepdims=True)
        acc_scr[...] = alpha * acc_scr[...] + _dot(p.astype(BF16), vblk)
        m_scr[...] = m_new
        return carry

    lax.fori_loop(0, i + 1, body, 0)
    out = acc_scr[...] / l_scr[...]
    for s in range(n_stack):
        o_ref[0, :, s * dv:(s + 1) * dv] = out[s * tq:(s + 1) * tq, :]


def _flash(q, k, v, n_kv_heads, n_stack, dk, dv, tq):
    batch, n, _ = q.shape
    tq = min(tq, n)
    v_from_k = v is None
    in_specs = [
        pl.BlockSpec((1, tq, n_stack * dk), lambda b, h, i: (b, i, h)),
        pl.BlockSpec((1, n, dk), lambda b, h, i: (b, 0, h)),
    ]
    args = [q, k]
    if not v_from_k:
        in_specs.append(pl.BlockSpec((1, n, dv), lambda b, h, i: (b, 0, h)))
        args.append(v)
    rows = n_stack * tq
    return pl.pallas_call(
        functools.partial(_flash_kernel, n_stack, dk, dv, v_from_k),
        grid=(batch, n_kv_heads, n // tq),
        in_specs=in_specs,
        out_specs=pl.BlockSpec((1, tq, n_stack * dv), lambda b, h, i: (b, i, h)),
        out_shape=jax.ShapeDtypeStruct((batch, n, n_kv_heads * n_stack * dv), F32),
        scratch_shapes=[pltpu.VMEM((rows, 1), F32), pltpu.VMEM((rows, 1), F32), pltpu.VMEM((rows, dv), F32)],
        compiler_params=_params("parallel", "parallel", "arbitrary"),
        name="flash",
    )(*args)


B_AUG = 6


def _log_sigmoid(z):
    return jnp.minimum(z, 0.0) - jnp.log1p(jnp.exp(-jnp.abs(z)))


def _split3(c):
    hi = c.astype(BF16)
    r = c - hi.astype(F32)
    mid = r.astype(BF16)
    lo = (r - mid.astype(F32)).astype(BF16)
    return hi, mid, lo


def _pad_heads(y):
    t, w = y.shape
    low = lax.broadcasted_iota(jnp.int32, (t, LANES), 1) < HEAD_DIM
    pieces = []
    for p in range(w // LANES):
        blk = y[:, p * LANES:(p + 1) * LANES]
        pieces.append(jnp.where(low, blk, 0.0))
        pieces.append(jnp.where(low, pltpu.roll(blk, HEAD_DIM, 1), 0.0))
    return jnp.concatenate(pieces, axis=1)


def _b_project_kernel(scale, n_heads, x_ref, g_ref, w_ref, wf_ref, bias_ref, pq_ref, pk_ref, oq_ref, ok_ref,
                      kv_ref, lf_ref, qa_ref, ka_ref, va_ref, carry_ref):
    @pl.when(pl.program_id(1) == 0)
    def _():
        carry_ref[...] = jnp.zeros_like(carry_ref)

    inner = n_heads * HEAD_DIM
    h = _rms(x_ref[...], g_ref[...]).astype(BF16)
    y = _dot(h, w_ref[...])
    z = _dot(h, wf_ref[...]) + bias_ref[...]
    tm = z.shape[0]
    lane = lax.broadcasted_iota(jnp.int32, (tm, LANES), 1)
    lf = jnp.where(lane < n_heads, _log_sigmoid(z), 0.0)
    lf_ref[...] = lf[:, :n_heads]
    tri = (lax.broadcasted_iota(jnp.int32, (tm, tm), 0) >= lax.broadcasted_iota(jnp.int32, (tm, tm), 1))
    tri = jnp.where(tri, 1.0, 0.0).astype(BF16)
    hi, mid, lo = _split3(lf)
    c = _dot(tri, hi) + _dot(tri, mid) + _dot(tri, lo) + carry_ref[...]
    carry_ref[...] = c[tm - 1:tm, :]
    c3 = jnp.concatenate(_split3(c), axis=1)
    q_aug = _dot(c3, pq_ref[...]) + oq_ref[...]
    k_aug = _dot(c3, pk_ref[...]) + ok_ref[...]
    q = y[:, :inner] * scale
    k = y[:, inner:2 * inner]
    v = y[:, 2 * inner:]
    kv_ref[:, :inner] = k
    kv_ref[:, inner:] = v
    qa_ref[...] = (_pad_heads(q) + q_aug).astype(BF16)
    ka_ref[...] = (_pad_heads(k) + k_aug).astype(BF16)
    va_ref[...] = _pad_heads(v).astype(BF16)


def _b_aug_constants(n_heads):
    pq = jnp.zeros((3 * LANES, n_heads * LANES), F32)
    pk = jnp.zeros((3 * LANES, n_heads * LANES), F32)
    oq = jnp.zeros((1, n_heads * LANES), F32)
    ok = jnp.zeros((1, n_heads * LANES), F32)
    heads = jnp.arange(n_heads)
    for piece in range(3):
        pq = pq.at[piece * LANES + heads, heads * LANES + HEAD_DIM + piece].set(1.0)
        pk = pk.at[piece * LANES + heads, heads * LANES + HEAD_DIM + 3 + piece].set(-1.0)
        ok = ok.at[0, heads * LANES + HEAD_DIM + piece].set(1.0)
        oq = oq.at[0, heads * LANES + HEAD_DIM + 3 + piece].set(1.0)
    return pq.astype(BF16), pk.astype(BF16), oq, ok


def _b_project(x, g, w_qkv, w_f, bias, batch, n, n_heads):
    m, d = x.shape
    inner = n_heads * HEAD_DIM
    tm = _tile(n, (256, 128, 64, 32, 16, 8))
    nt = n // tm
    pq, pk, oq, ok = _b_aug_constants(n_heads)
    row = lambda b, t: (b * nt + t, 0)
    fix = lambda b, t: (0, 0)
    wide = n_heads * LANES
    return pl.pallas_call(
        functools.partial(_b_project_kernel, HEAD_DIM ** -0.5, n_heads),
        grid=(batch, nt),
        in_specs=[
            pl.BlockSpec((tm, d), row),
            pl.BlockSpec((1, d), fix),
            pl.BlockSpec((d, 3 * inner), fix),
            pl.BlockSpec((d, LANES), fix),
            pl.BlockSpec((1, LANES), fix),
            pl.BlockSpec((3 * LANES, wide), fix),
            pl.BlockSpec((3 * LANES, wide), fix),
            pl.BlockSpec((1, wide), fix),
            pl.BlockSpec((1, wide), fix),
        ],
        out_specs=[
            pl.BlockSpec((tm, 2 * inner), row),
            pl.BlockSpec((tm, n_heads), row),
            pl.BlockSpec((tm, wide), row),
            pl.BlockSpec((tm, wide), row),
            pl.BlockSpec((tm, wide), row),
        ],
        out_shape=[
            jax.ShapeDtypeStruct((m, 2 * inner), F32),
            jax.ShapeDtypeStruct((m, n_heads), F32),
            jax.ShapeDtypeStruct((m, wide), BF16),
            jax.ShapeDtypeStruct((m, wide), BF16),
            jax.ShapeDtypeStruct((m, wide), BF16),
        ],
        scratch_shapes=[pltpu.VMEM((1, LANES), F32)],
        compiler_params=_params("parallel", "arbitrary"),
        name="b_project",
    )(x, g.reshape(1, d), w_qkv, w_f, bias, pq, pk, oq, ok)


def _online_update(s, m_scr, l_scr, acc_scr, pv):
    m_old = m_scr[...]
    m_new = jnp.maximum(m_old, jnp.max(s, axis=-1, keepdims=True))
    alpha = jnp.exp(m_old - m_new)
    p = jnp.exp(s - m_new)
    l_scr[...] = alpha * l_scr[...] + jnp.sum(p, axis=-1, keepdims=True)
    acc_scr[...] = alpha * acc_scr[...] + pv(p.astype(BF16))
    m_scr[...] = m_new


def _b_sample_kernel(scale, n_q, n_heads, pt_ref, y_ref, bias_ref, kt_ref, vt_ref, lft_ref, o_ref, lfo_ref,
                     qbd_scr, at_scr, m_scr, l_scr, acc_scr, tail_scr, kn_scr, vn_scr):
    p = pl.program_id(1)
    inner = n_heads * HEAD_DIM
    rows = n_q * n_heads
    slots = kt_ref.shape[3]

    @pl.when(p == 0)
    def _():
        y = y_ref[0]
        q = y[:, :inner] * scale
        z = y[:, 3 * inner:] + bias_ref[...]
        lane = lax.broadcasted_iota(jnp.int32, (n_q, LANES), 1)
        lf = jnp.where(lane < n_heads, _log_sigmoid(z), 0.0)
        lfo_ref[0] = lf
        run = [lf[0:1]]
        for i in range(1, n_q):
            run.append(run[-1] + lf[i:i + 1])
        eye = (lax.broadcasted_iota(jnp.int32, (n_heads, LANES), 0)
               == lax.broadcasted_iota(jnp.int32, (n_heads, LANES), 1))
        cols = [jnp.sum(jnp.where(eye, jnp.broadcast_to(r, (n_heads, LANES)), 0.0), axis=1, keepdims=True)
                for r in run]
        a_col = jnp.concatenate(cols, axis=0)
        at_scr[...] = a_col
        lane_head = lax.broadcasted_iota(jnp.int32, (n_heads, inner), 1) // HEAD_DIM
        own = lane_head == lax.broadcasted_iota(jnp.int32, (n_heads, inner), 0)
        qbd = jnp.concatenate(
            [jnp.where(own, jnp.broadcast_to(q[i:i + 1, :], (n_heads, inner)), 0.0) for i in range(n_q)],
            axis=0).astype(BF16)
        qbd_scr[...] = qbd
        kn_scr[...] = jnp.zeros_like(kn_scr)
        vn_scr[...] = jnp.zeros_like(vn_scr)
        kn_scr[0:n_q, :] = y[:, inner:2 * inner]
        vn_scr[0:n_q, :] = y[:, 2 * inner:3 * inner]
        s = _dot_nt(qbd, kn_scr[...].astype(BF16))
        col = lax.broadcasted_iota(jnp.int32, (rows, LANES), 1)
        a_key = jnp.zeros((rows, LANES), F32)
        for i in range(n_q):
            a_key = jnp.where(col == i, jnp.concatenate([cols[i]] * n_q, axis=0), a_key)
        s = s + a_col - a_key
        qi = lax.broadcasted_iota(jnp.int32, (rows, LANES), 0) // n_heads
        s = jnp.where(col <= qi, s, NEG)
        mx = jnp.max(s, axis=-1, keepdims=True)
        pr = jnp.exp(s - mx)
        m_scr[...] = mx
        l_scr[...] = jnp.sum(pr, axis=-1, keepdims=True)
        acc_scr[...] = _dot(pr.astype(BF16), vn_scr[...].astype(BF16))
        tail_scr[...] = jnp.zeros_like(tail_scr)

    lft = lft_ref[0]
    later = (lax.broadcasted_iota(jnp.int32, (slots, slots), 0) > lax.broadcasted_iota(jnp.int32, (slots, slots), 1))
    later = jnp.where(later, 1.0, 0.0).astype(BF16)
    r3 = _dot(jnp.concatenate(_split3(lft), axis=0), later)
    r = r3[:n_heads] + r3[n_heads:2 * n_heads] + r3[2 * n_heads:] + tail_scr[...]
    s = (_dot(qbd_scr[...], kt_ref[0, 0].astype(BF16)) + at_scr[...]
         + jnp.concatenate([r] * n_q, axis=0))
    vt = vt_ref[0, 0].astype(BF16)
    _online_update(s, m_scr, l_scr, acc_scr, lambda pb: _dot_nt(pb, vt))
    tail_scr[...] = tail_scr[...] + jnp.sum(lft, axis=1, keepdims=True)

    @pl.when(p == pl.num_programs(1) - 1)
    def _():
        lane_head = lax.broadcasted_iota(jnp.int32, (n_heads, inner), 1) // HEAD_DIM
        own = lane_head == lax.broadcasted_iota(jnp.int32, (n_heads, inner), 0)
        out = acc_scr[...] / l_scr[...]
        for i in range(n_q):
            o_ref[0, i:i + 1, :] = jnp.sum(jnp.where(own, out[i * n_heads:(i + 1) * n_heads], 0.0),
                                           axis=0, keepdims=True)


def _b_sample(y, bias, cache_kv, cache_logf, page_table, dec_batch, n_q, n_heads):
    inner = n_heads * HEAD_DIM
    pool, slots = cache_kv.shape[0], cache_kv.shape[1]
    n_pages = page_table.shape[1]
    kvt = jnp.transpose(cache_kv, (0, 2, 3, 4, 1)).reshape(pool, 2, inner, slots)
    lft = jnp.transpose(cache_logf, (0, 2, 1))
    yv = y.reshape(dec_batch, n_q, 3 * inner + LANES)
    rows = n_q * n_heads
    grid_spec = pltpu.PrefetchScalarGridSpec(
        num_scalar_prefetch=1,
        grid=(dec_batch, n_pages),
        in_specs=[
            pl.BlockSpec((1, n_q, 3 * inner + LANES), lambda b, p, pt: (b, 0, 0)),
            pl.BlockSpec((1, LANES), lambda b, p, pt: (0, 0)),
            pl.BlockSpec((1, 1, inner, slots), lambda b, p, pt: (pt[b, n_pages - 1 - p], 0, 0, 0)),
            pl.BlockSpec((1, 1, inner, slots), lambda b, p, pt: (pt[b, n_pages - 1 - p], 1, 0, 0)),
            pl.BlockSpec((1, n_heads, slots), lambda b, p, pt: (pt[b, n_pages - 1 - p], 0, 0)),
        ],
        out_specs=[
            pl.BlockSpec((1, n_q, inner), lambda b, p, pt: (b, 0, 0)),
            pl.BlockSpec((1, n_q, LANES), lambda b, p, pt: (b, 0, 0)),
        ],
        scratch_shapes=[
            pltpu.VMEM((rows, inner), BF16),
            pltpu.VMEM((rows, 1), F32),
            pltpu.VMEM((rows, 1), F32),
            pltpu.VMEM((rows, 1), F32),
            pltpu.VMEM((rows, inner), F32),
            pltpu.VMEM((n_heads, 1), F32),
            pltpu.VMEM((LANES, inner), F32),
            pltpu.VMEM((LANES, inner), F32),
        ],
    )
    o, lf = pl.pallas_call(
        functools.partial(_b_sample_kernel, HEAD_DIM ** -0.5, n_q, n_heads),
        grid_spec=grid_spec,
        out_shape=[jax.ShapeDtypeStruct((dec_batch, n_q, inner), F32),
                   jax.ShapeDtypeStruct((dec_batch, n_q, LANES), F32)],
        compiler_params=_params("parallel", "arbitrary"),
        name="b_sample",
    )(page_table, yv, bias, kvt, kvt, lft)
    return o.reshape(dec_batch * n_q, inner), lf[:, :, :n_heads]


def _mixer_b(xp, xs, g, w_in, f_bias, w_out, cache_kv, cache_logf, page_table, batch, n, dec_batch, n_q):
    inner = w_out.shape[0]
    n_heads = inner // HEAD_DIM
    d = xp.shape[1]
    w_qkv = w_in[:, :3 * inner].astype(BF16)
    w_f = jnp.pad(w_in[:, 3 * inner:], ((0, 0), (0, LANES - n_heads))).astype(BF16)
    bias = jnp.pad(f_bias.reshape(1, n_heads), ((0, 0), (0, LANES - n_heads)))
    w_out_pad = jnp.pad(w_out.reshape(n_heads, HEAD_DIM, d),
                        ((0, 0), (0, LANES - HEAD_DIM), (0, 0))).reshape(n_heads * LANES, d).astype(BF16)

    kv_p, lf_p, qa, ka, va = _b_project(xp, g, w_qkv, w_f, bias, batch, n, n_heads)
    wide = n_heads * LANES
    o_p = _flash(qa.reshape(batch, n, wide), ka.reshape(batch, n, wide), va.reshape(batch, n, wide),
                 n_heads, 1, LANES, LANES, 256)
    xp = _outproj(xp, o_p.reshape(batch * n, wide), w_out_pad)

    y_s = _rms_matmul(xs, g, jnp.concatenate([w_qkv, w_f], axis=1))
    o_s, lf_s = _b_sample(y_s, bias, cache_kv, cache_logf, page_table, dec_batch, n_q, n_heads)
    xs = _outproj(xs, o_s, w_out.astype(BF16))
    kv_s = y_s[:, inner:3 * inner].reshape(dec_batch, n_q, 2, n_heads, HEAD_DIM)
    return (xp, xs, kv_p.reshape(batch, n, 2, n_heads, HEAD_DIM), lf_p.reshape(batch, n, n_heads), kv_s, lf_s)


def _c_project_kernel(scale, n_heads, x_ref, g_ref, wq_ref, wkv_ref, wpe_ref, qn_ref, kvn_ref, wqn_ref, wqr_ref,
                      wuk_ref, cos_ref, sin_ref, st_ref, kc_ref, qa_ref):
    h = _rms(x_ref[...], g_ref[...]).astype(BF16)
    cq = _rms(_dot(h, wq_ref[...]), qn_ref[...]).astype(BF16)
    ckv = _rms(_dot(h, wkv_ref[...]), kvn_ref[...])
    cos, sin = cos_ref[...], sin_ref[...]
    ype = _dot(h, wpe_ref[...])
    kpe = ype * cos + _swap_halves(ype, C_ROPE // 2) * sin
    lat = ckv.shape[1]
    st_ref[:, :lat] = ckv
    st_ref[:, lat:lat + C_ROPE] = kpe[:, :C_ROPE]
    kc_ref[:, :lat] = ckv.astype(BF16)
    kc_ref[:, lat:] = kpe.astype(BF16)
    qn = _dot(cq, wqn_ref[...]).astype(BF16)
    qr = _dot(cq, wqr_ref[...])
    dq = lat + LANES
    for pair in range(n_heads // 2):
        ql = _dot(qn[:, pair * LANES:(pair + 1) * LANES], wuk_ref[pair])
        for j in range(2):
            head = 2 * pair + j
            r = qr[:, head * LANES:(head + 1) * LANES]
            r = r * cos + _swap_halves(r, C_ROPE // 2) * sin
            qa_ref[:, head * dq:head * dq + lat] = (ql[:, j * lat:(j + 1) * lat] * scale).astype(BF16)
            qa_ref[:, head * dq + lat:(head + 1) * dq] = (r * scale).astype(BF16)


def _c_project(x, g, w, q_norm, kv_norm, cos_t, sin_t, n_heads):
    m, d = x.shape
    lat = w["kv"].shape[1]
    q_lora = w["q"].shape[1]
    p = cos_t.shape[0]
    tm = _tile(p, (256, 128, 64, 32, 16, 8))
    npb = p // tm
    dq = lat + LANES
    row = lambda i: (i, 0)
    fix = lambda i: (0, 0)
    tab = lambda i: (i % npb, 0)
    return pl.pallas_call(
        functools.partial(_c_project_kernel, (C_NOPE + C_ROPE) ** -0.5, n_heads),
        grid=(m // tm,),
        in_specs=[
            pl.BlockSpec((tm, d), row),
            pl.BlockSpec((1, d), fix),
            pl.BlockSpec((d, q_lora), fix),
            pl.BlockSpec((d, lat), fix),
            pl.BlockSpec((d, LANES), fix),
            pl.BlockSpec((1, q_lora), fix),
            pl.BlockSpec((1, lat), fix),
            pl.BlockSpec((q_lora, n_heads * C_NOPE), fix),
            pl.BlockSpec((q_lora, n_heads * LANES), fix),
            pl.BlockSpec((n_heads // 2, LANES, 2 * lat), lambda i: (0, 0, 0)),
            pl.BlockSpec((tm, LANES), tab),
            pl.BlockSpec((tm, LANES), tab),
        ],
        out_specs=[
            pl.BlockSpec((tm, lat + C_ROPE), row),
            pl.BlockSpec((tm, dq), row),
            pl.BlockSpec((tm, n_heads * dq), row),
        ],
        out_shape=[
            jax.ShapeDtypeStruct((m, lat + C_ROPE), F32),
            jax.ShapeDtypeStruct((m, dq), BF16),
            jax.ShapeDtypeStruct((m, n_heads * dq), BF16),
        ],
        compiler_params=_params("parallel"),
        name="c_project",
    )(x, g.reshape(1, d), w["q"], w["kv"], w["pe"], q_norm.reshape(1, -1), kv_norm.reshape(1, -1),
      w["qn"], w["qr"], w["uk"], cos_t, sin_t)


def _c_sample_kernel(n_q, n_heads, lat, pt_ref, q_ref, kn_ref, page_ref, o_ref,
                     m_scr, l_scr, acc_scr, kn_scr, kr_scr):
    p = pl.program_id(1)
    rows = n_q * n_heads
    q = q_ref[0]

    @pl.when(p == 0)
    def _():
        kn_scr[...] = jnp.zeros_like(kn_scr)
        kr_scr[...] = jnp.zeros_like(kr_scr)
        kn_scr[0:n_q, :] = kn_ref[0].astype(F32)
        knew = kn_scr[...].astype(BF16)
        s = _dot_nt(q, knew)
        col = lax.broadcasted_iota(jnp.int32, (rows, LANES), 1)
        qi = lax.broadcasted_iota(jnp.int32, (rows, LANES), 0) // n_heads
        s = jnp.where(col <= qi, s, NEG)
        mx = jnp.max(s, axis=-1, keepdims=True)
        pr = jnp.exp(s - mx)
        m_scr[...] = mx
        l_scr[...] = jnp.sum(pr, axis=-1, keepdims=True)
        acc_scr[...] = _dot(pr.astype(BF16), knew[:, :lat])

    page = page_ref[0]
    lat_t = page[:lat].astype(BF16)
    kr_scr[0:C_ROPE, :] = page[lat:lat + C_ROPE]
    s = _dot(q[:, :lat], lat_t) + _dot(q[:, lat:], kr_scr[...].astype(BF16))
    _online_update(s, m_scr, l_scr, acc_scr, lambda pb: _dot_nt(pb, lat_t))

    @pl.when(p == pl.num_programs(1) - 1)
    def _():
        o_ref[0] = acc_scr[...] / l_scr[...]


def _c_sample(q, kn, cache, page_table, dec_batch, n_q, n_heads, lat):
    pool, slots, width = cache.shape
    n_pages = page_table.shape[1]
    cache_t = jnp.transpose(cache, (0, 2, 1))
    rows = n_q * n_heads
    dq = lat + LANES
    grid_spec = pltpu.PrefetchScalarGridSpec(
        num_scalar_prefetch=1,
        grid=(dec_batch, n_pages),
        in_specs=[
            pl.BlockSpec((1, rows, dq), lambda b, p, pt: (b, 0, 0)),
            pl.BlockSpec((1, n_q, dq), lambda b, p, pt: (b, 0, 0)),
            pl.BlockSpec((1, width, slots), lambda b, p, pt: (pt[b, p], 0, 0)),
        ],
        out_specs=pl.BlockSpec((1, rows, lat), lambda b, p, pt: (b, 0, 0)),
        scratch_shapes=[
            pltpu.VMEM((rows, 1), F32),
            pltpu.VMEM((rows, 1), F32),
            pltpu.VMEM((rows, lat), F32),
            pltpu.VMEM((LANES, dq), F32),
            pltpu.VMEM((LANES, slots), F32),
        ],
    )
    return pl.pallas_call(
        functools.partial(_c_sample_kernel, n_q, n_heads, lat),
        grid_spec=grid_spec,
        out_shape=jax.ShapeDtypeStruct((dec_batch, rows, lat), F32),
        compiler_params=_params("parallel", "arbitrary"),
        name="c_sample",
    )(page_table, q, kn, cache_t)


def _c_up_kernel(n_pairs, x_ref, ol_ref, wuv_ref, wo_ref, y_ref):
    ol = ol_ref[...]
    w = ol.shape[1] // n_pairs
    o = jnp.concatenate([_dot(ol[:, p * w:(p + 1) * w].astype(BF16), wuv_ref[p]) for p in range(n_pairs)], axis=1)
    y_ref[...] = x_ref[...] + _dot(o.astype(BF16), wo_ref[...])


def _c_up(x, o_lat, w_uv, w_out):
    m, d = x.shape
    k = o_lat.shape[1]
    tm = _tile(m, (256, 128, 64, 32, 16, 8))
    return pl.pallas_call(
        functools.partial(_c_up_kernel, w_uv.shape[0]),
        grid=(m // tm,),
        in_specs=[
            pl.BlockSpec((tm, d), lambda i: (i, 0)),
            pl.BlockSpec((tm, k), lambda i: (i, 0)),
            pl.BlockSpec(w_uv.shape, lambda i: (0, 0, 0)),
            pl.BlockSpec(w_out.shape, lambda i: (0, 0)),
        ],
        out_specs=pl.BlockSpec((tm, d), lambda i: (i, 0)),
        out_shape=jax.ShapeDtypeStruct((m, d), F32),
        compiler_params=_params("parallel"),
        name="c_up",
    )(x, o_lat, w_uv, w_out)


def _mixer_c(xp, xs, g, w_in, q_norm, kv_norm, w_qb, w_kvb, w_out, cache, page_table,
             batch, n, dec_batch, n_q, past_len):
    q_lora, lat = q_norm.shape[0], kv_norm.shape[0]
    n_heads = w_out.shape[0] // C_V
    dq = lat + LANES
    qb3 = w_qb.reshape(q_lora, n_heads, C_NOPE + C_ROPE)
    kvb3 = w_kvb.reshape(lat, n_heads, C_NOPE + C_V)
    uk_t = jnp.transpose(kvb3[..., :C_NOPE], (1, 2, 0))
    uk = jnp.zeros((n_heads // 2, LANES, 2 * lat), F32)
    uk = uk.at[:, :C_NOPE, :lat].set(uk_t[0::2]).at[:, C_NOPE:, lat:].set(uk_t[1::2])
    uv_h = jnp.transpose(kvb3[..., C_NOPE:], (1, 0, 2))
    uv = jnp.zeros((n_heads // 2, 2 * lat, LANES), F32)
    uv = uv.at[:, :lat, :C_V].set(uv_h[0::2]).at[:, lat:, C_V:].set(uv_h[1::2])
    w = {
        "q": w_in[:, :q_lora].astype(BF16),
        "kv": w_in[:, q_lora:q_lora + lat].astype(BF16),
        "pe": jnp.pad(w_in[:, q_lora + lat:], ((0, 0), (0, LANES - C_ROPE))).astype(BF16),
        "qn": qb3[..., :C_NOPE].reshape(q_lora, n_heads * C_NOPE).astype(BF16),
        "qr": jnp.pad(qb3[..., C_NOPE:], ((0, 0), (0, 0), (0, LANES - C_ROPE))).reshape(q_lora, n_heads * LANES).astype(BF16),
        "uk": uk.astype(BF16),
    }
    uv = uv.astype(BF16)
    w_o = w_out.astype(BF16)

    cos_p, sin_p = _rope_tables(jnp.arange(n), C_ROPE, 1, pad_to=LANES)
    st_p, kc_p, qa_p = _c_project(xp, g, w, q_norm, kv_norm, cos_p, sin_p, n_heads)
    o_lat_p = _flash(qa_p.reshape(batch, n, n_heads * dq), kc_p.reshape(batch, n, dq), None,
                     1, n_heads, dq, lat, 128)
    xp = _c_up(xp, o_lat_p.reshape(batch * n, n_heads * lat), uv, w_o)

    cos_s, sin_s = _rope_tables(past_len + jnp.arange(n_q), C_ROPE, 1, pad_to=LANES)
    cos_s, sin_s = jnp.tile(cos_s, (dec_batch, 1)), jnp.tile(sin_s, (dec_batch, 1))
    st_s, kc_s, qa_s = _c_project(xs, g, w, q_norm, kv_norm, cos_s, sin_s, n_heads)
    o_lat_s = _c_sample(qa_s.reshape(dec_batch, n_q * n_heads, dq), kc_s.reshape(dec_batch, n_q, dq),
                        cache, page_table, dec_batch, n_q, n_heads, lat)
    o_lat_s = o_lat_s.reshape(dec_batch * n_q, n_heads * lat)
    xs = _c_up(xs, o_lat_s, uv, w_o)
    return xp, xs, st_p.reshape(batch, n, lat + C_ROPE), st_s.reshape(dec_batch, n_q, lat + C_ROPE)


def kernel(x_prompt, x_sample, cache_a_kv0, cache_a_kv1, cache_a_kv2, cache_b_kv, cache_b_logf, cache_c_ckv, page_table, p_prompt, p_sample, norm_g, final_g, ffn_w_in, ffn_w_out, ple_w_gate, ple_w_proj, a_w_in, a_w_out, b_w_in, b_f_bias, b_w_out, c_w_in, c_q_norm, c_kv_norm, c_w_qb, c_w_kvb, c_w_out):
    batch, n, d = x_prompt.shape
    dec_batch, n_q, _ = x_sample.shape
    depth = norm_g.shape[0]
    past_len = page_table.shape[1] * cache_b_kv.shape[2]
    xp = x_prompt.reshape(batch * n, d)
    xs = x_sample.reshape(dec_batch * n_q, d)
    a_bufs = (cache_a_kv0, cache_a_kv1, cache_a_kv2)
    a_new_p = [[] for _ in A_GROUPS]
    a_new_s = [[] for _ in A_GROUPS]
    b_kv_p, b_lf_p, b_kv_s, b_lf_s, c_p, c_s = [], [], [], [], [], []
    for i in range(depth):
        kind, j = i % N_MIXERS, i // N_MIXERS
        w1, w2 = ffn_w_in[i, 0].astype(BF16), ffn_w_out[i, 0].astype(BF16)
        xp = _ffn(xp, norm_g[i, 0], w1, w2)
        xs = _ffn(xs, norm_g[i, 0], w1, w2)
        if kind == 0:
            xp, xs, st_p, st_s = _mixer_a(xp, xs, norm_g[i, 1], a_w_in[j].astype(BF16), a_w_out[j].astype(BF16),
                                          [buf[j] for buf in a_bufs], batch, n, dec_batch, n_q, past_len)
            for g in range(len(A_GROUPS)):
                a_new_p[g].append(st_p[g])
                a_new_s[g].append(st_s[g])
        elif kind == 1:
            xp, xs, kvp, lfp, kvs, lfs = _mixer_b(xp, xs, norm_g[i, 1], b_w_in[j], b_f_bias[j], b_w_out[j],
                                                  cache_b_kv[j], cache_b_logf[j], page_table,
                                                  batch, n, dec_batch, n_q)
            b_kv_p.append(kvp)
            b_lf_p.append(lfp)
            b_kv_s.append(kvs)
            b_lf_s.append(lfs)
        else:
            xp, xs, cp, cs = _mixer_c(xp, xs, norm_g[i, 1], c_w_in[j], c_q_norm[j], c_kv_norm[j], c_w_qb[j],
                                      c_w_kvb[j], c_w_out[j], cache_c_ckv[j], page_table,
                                      batch, n, dec_batch, n_q, past_len)
            c_p.append(cp)
            c_s.append(cs)
        w1, w2 = ffn_w_in[i, 1].astype(BF16), ffn_w_out[i, 1].astype(BF16)
        xp = _ffn(xp, norm_g[i, 2], w1, w2)
        xs = _ffn(xs, norm_g[i, 2], w1, w2)
        wg, wp = ple_w_gate[i].astype(BF16), ple_w_proj[i].astype(BF16)
        fg = final_g if i == depth - 1 else None
        xp = _ple(xp, p_prompt[i].reshape(batch * n, -1), norm_g[i, 3], wg, wp, fg)
        xs = _ple(xs, p_sample[i].reshape(dec_batch * n_q, -1), norm_g[i, 3], wg, wp, fg)
    return (xp.reshape(batch, n, d), xs.reshape(dec_batch, n_q, d),
            jnp.stack(a_new_p[0]), jnp.stack(a_new_p[1]), jnp.stack(a_new_p[2]),
            jnp.stack(b_kv_p), jnp.stack(b_lf_p), jnp.stack(c_p),
            jnp.stack(a_new_s[0]), jnp.stack(a_new_s[1]), jnp.stack(a_new_s[2]),
            jnp.stack(b_kv_s), jnp.stack(b_lf_s), jnp.stack(c_s))
```

```python
import functools

import jax
import jax.numpy as jnp
from jax import lax
from jax.experimental import pallas as pl
from jax.experimental.pallas import tpu as pltpu

F32 = jnp.float32
BF16 = jnp.bfloat16

EPS = 1e-6
NEG = -1e30
ROPE_THETA = 10000.0
N_MIXERS = 3
A_GROUPS = ((128, 1), (512, 4), (2048, 16))
HEAD_DIM = 64
C_NOPE, C_ROPE, C_V = 64, 32, 64

LANES = 128
VMEM_LIMIT_BYTES = 56 * 1024 * 1024


def _params(*semantics):
    return pltpu.CompilerParams(dimension_semantics=semantics,
                                vmem_limit_bytes=VMEM_LIMIT_BYTES)


def _tile(m, candidates):
    for c in candidates:
        if m % c == 0:
            return c
    return m


def _rms(x, g):
    ms = jnp.mean(x * x, axis=-1, keepdims=True)
    return x * lax.rsqrt(ms + EPS) * g


def _dot(a, b):
    return jnp.dot(a, b, preferred_element_type=F32)


def _dot_nt(a, b):
    return lax.dot_general(a, b, (((1,), (1,)), ((), ())), preferred_element_type=F32)


def _swap_halves(y, half):
    width = y.shape[-1]
    lane = lax.broadcasted_iota(jnp.int32, y.shape, y.ndim - 1)
    first = (lane % (2 * half)) < half
    return jnp.where(first, pltpu.roll(y, width - half, y.ndim - 1), pltpu.roll(y, half, y.ndim - 1))


def _rope_tables(pos, dim, reps, pad_to=None):
    half = dim // 2
    inv = ROPE_THETA ** (-jnp.arange(half, dtype=F32) * 2.0 / dim)
    ang = pos.astype(F32)[:, None] * inv[None, :]
    cos, sin = jnp.cos(ang), jnp.sin(ang)
    c = jnp.concatenate([cos, cos], axis=-1)
    s = jnp.concatenate([-sin, sin], axis=-1)
    if pad_to is not None:
        z = jnp.zeros((pos.shape[0], pad_to - dim), F32)
        c = jnp.concatenate([c, z], axis=-1)
        s = jnp.concatenate([s, z], axis=-1)
    return jnp.tile(c, (1, reps)), jnp.tile(s, (1, reps))


def _ffn_kernel(x_ref, g_ref, wg_ref, wu_ref, wo_ref, o_ref, h_ref, acc_ref):
    j = pl.program_id(1)

    @pl.when(j == 0)
    def _():
        h_ref[...] = _rms(x_ref[...], g_ref[...]).astype(BF16)
        acc_ref[...] = jnp.zeros_like(acc_ref)

    h = h_ref[...]
    gate = _dot(h, wg_ref[...])
    up = _dot(h, wu_ref[...])
    act = (gate * jax.nn.sigmoid(gate) * up).astype(BF16)
    acc_ref[...] += _dot(act, wo_ref[...])

    @pl.when(j == pl.num_programs(1) - 1)
    def _():
        o_ref[...] = x_ref[...] + 0.5 * acc_ref[...]


def _ffn(x, g, w_in, w_out):
    m, d = x.shape
    f = w_out.shape[0]
    tm = _tile(m, (1024, 512, 256, 128, 64, 32, 16, 8))
    tf = _tile(f, (256, 128))
    nf = f // tf
    return pl.pallas_call(
        _ffn_kernel,
        grid=(m // tm, nf),
        in_specs=[
            pl.BlockSpec((tm, d), lambda i, j: (i, 0)),
            pl.BlockSpec((1, d), lambda i, j: (0, 0)),
            pl.BlockSpec((d, tf), lambda i, j: (0, j)),
            pl.BlockSpec((d, tf), lambda i, j: (0, j + nf)),
            pl.BlockSpec((tf, d), lambda i, j: (j, 0)),
        ],
        out_specs=pl.BlockSpec((tm, d), lambda i, j: (i, 0)),
        out_shape=jax.ShapeDtypeStruct((m, d), F32),
        scratch_shapes=[pltpu.VMEM((tm, d), BF16), pltpu.VMEM((tm, d), F32)],
        compiler_params=_params("parallel", "arbitrary"),
        name="ffn_half",
    )(x, g.reshape(1, d), w_in, w_in, w_out)


def _ple_kernel(x_ref, p_ref, g_ref, wg_ref, wp_ref, o_ref):
    x = x_ref[...]
    gate = jax.nn.sigmoid(_dot(_rms(x, g_ref[...]).astype(BF16), wg_ref[...]))
    o_ref[...] = x + gate * _dot(p_ref[...].astype(BF16), wp_ref[...])


def _ple_final_kernel(x_ref, p_ref, g_ref, wg_ref, wp_ref, fg_ref, o_ref):
    x = x_ref[...]
    gate = jax.nn.sigmoid(_dot(_rms(x, g_ref[...]).astype(BF16), wg_ref[...]))
    y = x + gate * _dot(p_ref[...].astype(BF16), wp_ref[...])
    o_ref[...] = _rms(y, fg_ref[...])


def _ple(x, p, g, w_gate, w_proj, final_g=None):
    m, d = x.shape
    pd = p.shape[1]
    tm = _tile(m, (512, 256, 128, 64, 32, 16, 8))
    row = lambda i: (i, 0)
    fix = lambda i: (0, 0)
    in_specs = [
        pl.BlockSpec((tm, d), row),
        pl.BlockSpec((tm, pd), row),
        pl.BlockSpec((1, d), fix),
        pl.BlockSpec((d, d), fix),
        pl.BlockSpec((pd, d), fix),
    ]
    args = [x, p, g.reshape(1, d), w_gate, w_proj]
    body = _ple_kernel
    if final_g is not None:
        in_specs.append(pl.BlockSpec((1, d), fix))
        args.append(final_g.reshape(1, d))
        body = _ple_final_kernel
    return pl.pallas_call(
        body,
        grid=(m // tm,),
        in_specs=in_specs,
        out_specs=pl.BlockSpec((tm, d), row),
        out_shape=jax.ShapeDtypeStruct((m, d), F32),
        compiler_params=_params("parallel"),
        name="ple",
    )(*args)


def _outproj_kernel(x_ref, o_ref_in, w_ref, y_ref):
    y_ref[...] = x_ref[...] + _dot(o_ref_in[...].astype(BF16), w_ref[...])


def _outproj(x, o, w):
    m, d = x.shape
    k = o.shape[1]
    tm = _tile(m, (512, 256, 128, 64, 32, 16, 8))
    return pl.pallas_call(
        _outproj_kernel,
        grid=(m // tm,),
        in_specs=[
            pl.BlockSpec((tm, d), lambda i: (i, 0)),
            pl.BlockSpec((tm, k), lambda i: (i, 0)),
            pl.BlockSpec((k, d), lambda i: (0, 0)),
        ],
        out_specs=pl.BlockSpec((tm, d), lambda i: (i, 0)),
        out_shape=jax.ShapeDtypeStruct((m, d), F32),
        compiler_params=_params("parallel"),
        name="outproj",
    )(x, o, w)


def _band_valid(tq, nk, offset, n_back):
    row = lax.broadcasted_iota(jnp.int32, (tq, nk), 0)
    col = lax.broadcasted_iota(jnp.int32, (tq, nk), 1)
    diff = row + offset - col
    return lax.bitcast_convert_type(diff, jnp.uint32) <= jnp.uint32(n_back)


def _a_combine_kernel(n_groups, *refs):
    x_ref = refs[0]
    o_refs = refs[1:1 + n_groups]
    l_refs = refs[1 + n_groups:1 + 2 * n_groups]
    w_ref, y_ref = refs[1 + 2 * n_groups:]
    lses = [r[...] for r in l_refs]
    mx = functools.reduce(jnp.maximum, lses)
    es = [jnp.exp(l - mx) for l in lses]
    den = functools.reduce(lambda a, b: a + b, es)
    num = functools.reduce(lambda a, b: a + b, [e * r[...] for e, r in zip(es, o_refs)])
    y_ref[...] = x_ref[...] + _dot((num / den).astype(BF16), w_ref[...])


def _a_combine(x, outs, lses, w_out):
    m, d = x.shape
    inner = w_out.shape[0]
    n_groups = len(outs)
    tm = _tile(m, (256, 128, 64, 32, 16, 8))
    row = lambda i: (i, 0)
    return pl.pallas_call(
        functools.partial(_a_combine_kernel, n_groups),
        grid=(m // tm,),
        in_specs=[pl.BlockSpec((tm, d), row)] + [pl.BlockSpec((tm, inner), row)] * (2 * n_groups)
        + [pl.BlockSpec((inner, d), lambda i: (0, 0))],
        out_specs=pl.BlockSpec((tm, d), row),
        out_shape=jax.ShapeDtypeStruct((m, d), F32),
        compiler_params=_params("parallel"),
        name="a_combine",
    )(x, *outs, *lses, w_out)


A_CHUNK = 512


def _rope_rows(x, cos, sin, n_heads):
    half = HEAD_DIM // 2
    parts = []
    for h in range(n_heads):
        parts.append(x[h * HEAD_DIM + half:(h + 1) * HEAD_DIM])
        parts.append(x[h * HEAD_DIM:h * HEAD_DIM + half])
    swapped = jnp.concatenate(parts, axis=0)
    return x * jnp.tile(cos, (n_heads, 1)) + swapped * jnp.tile(sin, (n_heads, 1))


def _rope_lanes(y, cos, sin):
    reps = y.shape[1] // LANES
    return y * jnp.tile(cos, (1, reps)) + _swap_halves(y, HEAD_DIM // 2) * jnp.tile(sin, (1, reps))


def _a_state_kernel(n_heads, x_ref, g_ref, wk_ref, wv_ref, cos_ref, sin_ref, *refs):
    o_ref = refs[-1]
    h = _rms(x_ref[...], g_ref[...]).astype(BF16)
    o_ref[0, 0, 0] = _rope_rows(_dot_nt(wk_ref[...], h), cos_ref[...], sin_ref[...], n_heads)
    o_ref[0, 0, 1] = _dot_nt(wv_ref[...], h)


def _a_state(x, g, wk_t, wv_t, cos_t, sin_t, prev, layer, n_layers, batch, n, window, n_heads):
    m, d = x.shape
    inner = n_heads * HEAD_DIM
    keep = min(window, n)
    tt = _tile(keep, (512, 256, 128))
    assert (n - keep) % tt == 0
    off = (n - keep) // tt
    fix = lambda b, t: (0, 0)
    in_specs = [
        pl.BlockSpec((tt, d), lambda b, t: (b * (n // tt) + off + t, 0)),
        pl.BlockSpec((1, d), fix),
        pl.BlockSpec((inner, d), fix),
        pl.BlockSpec((inner, d), fix),
        pl.BlockSpec((HEAD_DIM, tt), lambda b, t: (0, off + t)),
        pl.BlockSpec((HEAD_DIM, tt), lambda b, t: (0, off + t)),
    ]
    args = [x, g.reshape(1, d), wk_t, wv_t, cos_t, sin_t]
    aliases = {}
    if prev is not None:
        in_specs.append(pl.BlockSpec(memory_space=pl.ANY))
        args.append(prev)
        aliases = {len(args) - 1: 0}
    return pl.pallas_call(
        functools.partial(_a_state_kernel, n_heads),
        grid=(batch, keep // tt),
        in_specs=in_specs,
        out_specs=pl.BlockSpec((1, 1, 2, inner, tt), lambda b, t: (layer, b, 0, 0, t)),
        out_shape=jax.ShapeDtypeStruct((n_layers, batch, 2, inner, keep), F32),
        input_output_aliases=aliases,
        compiler_params=_params("parallel", "arbitrary"),
        name="a_state",
    )(*args)


def _a_prompt_kernel(scale, n_heads, n_back, halo, x_ref, *refs):
    if halo:
        (xh_ref, g_ref, wq_ref, wk_ref, wv_ref, cq_ref, sq_ref, ck_ref, sk_ref, ckh_ref, skh_ref,
         o_ref, lse_ref, q_scr, kt_scr, vt_scr) = refs
    else:
        (g_ref, wq_ref, wk_ref, wv_ref, cq_ref, sq_ref, ck_ref, sk_ref,
         o_ref, lse_ref, q_scr, kt_scr, vt_scr) = refs
    c = pl.program_id(2)
    chunk = x_ref.shape[1]
    inner = n_heads * HEAD_DIM
    h = _rms(x_ref[0], g_ref[...]).astype(BF16)
    q_scr[...] = (_rope_lanes(_dot(h, wq_ref[...]), cq_ref[...], sq_ref[...]) * scale).astype(BF16)
    kt_scr[:, halo:] = _rope_rows(_dot_nt(wk_ref[...], h), ck_ref[0], sk_ref[0], n_heads).astype(BF16)
    vt_scr[:, halo:] = _dot_nt(wv_ref[...], h).astype(BF16)
    if halo:
        hh = _rms(xh_ref[0], g_ref[...]).astype(BF16)
        kt_scr[:, :halo] = _rope_rows(_dot_nt(wk_ref[...], hh), ckh_ref[0], skh_ref[0], n_heads).astype(BF16)
        vt_scr[:, :halo] = _dot_nt(wv_ref[...], hh).astype(BF16)
    tq = min(LANES, chunk)
    low = lax.broadcasted_iota(jnp.int32, (tq, LANES), 1) < HEAD_DIM
    first_real = jnp.where(c > 0, 0, halo)

    def pair(hp, carry):
        r0 = pl.multiple_of(hp * LANES, LANES)
        for j in range(chunk // tq):
            q0 = j * tq
            if halo:
                k0, nk, offset = q0, tq + halo, halo
            else:
                k0 = max(0, q0 - n_back)
                nk, offset = q0 + tq - k0, q0 - k0
            valid = _band_valid(tq, nk, offset, n_back)
            qb = q_scr[q0:q0 + tq, pl.ds(r0, LANES)]
            kseg = kt_scr[pl.ds(r0, LANES), k0:k0 + nk]
            vseg = vt_scr[pl.ds(r0, LANES), k0:k0 + nk]
            outs, lses = [], []
            for head_low in (True, False):
                sel = low if head_low else jnp.logical_not(low)
                s = jnp.where(valid, _dot(jnp.where(sel, qb, jnp.zeros_like(qb)), kseg), NEG)
                if halo and j == 0:
                    s = jnp.where(lax.broadcasted_iota(jnp.int32, (tq, nk), 1) >= first_real, s, NEG)
                mx = jnp.max(s, axis=-1, keepdims=True)
                p = jnp.exp(s - mx)
                den = jnp.sum(p, axis=-1, keepdims=True)
                outs.append(_dot_nt(p.astype(BF16), vseg) / den)
                lses.append(mx + jnp.log(den))
            o_ref[0, q0:q0 + tq, pl.ds(r0, LANES)] = jnp.where(low, outs[0], outs[1])
            lse_ref[0, q0:q0 + tq, pl.ds(r0, LANES)] = jnp.where(low, lses[0], lses[1])
        return carry

    lax.fori_loop(0, inner // LANES, pair, 0)


def _a_prompt_group(x, g, wq, wk_t, wv_t, cos_l, sin_l, cos_r, sin_r, batch, n, gi, n_heads):
    window, dil = A_GROUPS[gi]
    n_back = window // dil
    d = x.shape[1]
    inner = n_heads * HEAD_DIM
    length = n // dil
    chunk = min(A_CHUNK, length)
    n_chunks = length // chunk
    halo = n_back if n_chunks > 1 else 0
    assert length % chunk == 0 and (halo == 0 or chunk % halo == 0)
    xv = x.reshape(batch, length, dil * d)
    clv = cos_l.reshape(length, dil * LANES)
    slv = sin_l.reshape(length, dil * LANES)
    fix = lambda b, r, c: (0, 0)
    prev = lambda c: jnp.maximum(c * (chunk // max(halo, 1)) - 1, 0)
    in_specs = [pl.BlockSpec((1, chunk, d), lambda b, r, c: (b, c, r))]
    args = [xv]
    if halo:
        in_specs.append(pl.BlockSpec((1, halo, d), lambda b, r, c: (b, prev(c), r)))
        args.append(xv)
    in_specs += [
        pl.BlockSpec((1, d), fix),
        pl.BlockSpec((d, inner), fix),
        pl.BlockSpec((inner, d), fix),
        pl.BlockSpec((inner, d), fix),
        pl.BlockSpec((chunk, LANES), lambda b, r, c: (c, r)),
        pl.BlockSpec((chunk, LANES), lambda b, r, c: (c, r)),
        pl.BlockSpec((1, HEAD_DIM, chunk), lambda b, r, c: (r, 0, c)),
        pl.BlockSpec((1, HEAD_DIM, chunk), lambda b, r, c: (r, 0, c)),
    ]
    args += [g.reshape(1, d), wq, wk_t, wv_t, clv, slv, cos_r, sin_r]
    if halo:
        in_specs += [pl.BlockSpec((1, HEAD_DIM, halo), lambda b, r, c: (r, 0, prev(c)))] * 2
        args += [cos_r, sin_r]
    out_spec = pl.BlockSpec((1, chunk, inner), lambda b, r, c: (b, c, r))
    shape = jax.ShapeDtypeStruct((batch, length, dil * inner), F32)
    o, lse = pl.pallas_call(
        functools.partial(_a_prompt_kernel, HEAD_DIM ** -0.5, n_heads, n_back, halo),
        grid=(batch, dil, n_chunks),
        in_specs=in_specs,
        out_specs=[out_spec, out_spec],
        out_shape=[shape, shape],
        scratch_shapes=[pltpu.VMEM((chunk, inner), BF16), pltpu.VMEM((inner, chunk + halo), BF16),
                        pltpu.VMEM((inner, chunk + halo), BF16)],
        compiler_params=_params("parallel", "parallel", "arbitrary"),
        name=f"a_prompt_g{gi}",
    )(*args)
    return o.reshape(batch * n, inner), lse.reshape(batch * n, inner)


def _a_sample_project_kernel(scale, n_heads, x_ref, g_ref, wq_ref, wk_ref, wv_ref, cq_ref, sq_ref, ck_ref, sk_ref,
                             q_ref, kv_ref):
    h = _rms(x_ref[...], g_ref[...]).astype(BF16)
    q_ref[...] = _rope_lanes(_dot(h, wq_ref[0]), cq_ref[...], sq_ref[...]) * scale
    kv_ref[0, 0] = _rope_rows(_dot_nt(wk_ref[0], h), ck_ref[...], sk_ref[...], n_heads)
    kv_ref[0, 1] = _dot_nt(wv_ref[0], h)


def _a_sample_project(x, g, wq, wk_t, wv_t, cos_l, sin_l, cos_r, sin_r, n_heads):
    t, d = x.shape
    n_groups = wq.shape[0]
    inner = n_heads * HEAD_DIM
    fix = lambda gi: (0, 0)
    return pl.pallas_call(
        functools.partial(_a_sample_project_kernel, HEAD_DIM ** -0.5, n_heads),
        grid=(n_groups,),
        in_specs=[
            pl.BlockSpec((t, d), fix),
            pl.BlockSpec((1, d), fix),
            pl.BlockSpec((1, d, inner), lambda gi: (gi, 0, 0)),
            pl.BlockSpec((1, inner, d), lambda gi: (gi, 0, 0)),
            pl.BlockSpec((1, inner, d), lambda gi: (gi, 0, 0)),
            pl.BlockSpec((t, LANES), fix),
            pl.BlockSpec((t, LANES), fix),
            pl.BlockSpec((HEAD_DIM, t), fix),
            pl.BlockSpec((HEAD_DIM, t), fix),
        ],
        out_specs=[pl.BlockSpec((t, inner), lambda gi: (0, gi)),
                   pl.BlockSpec((1, 2, inner, t), lambda gi: (gi, 0, 0, 0))],
        out_shape=[jax.ShapeDtypeStruct((t, n_groups * inner), F32),
                   jax.ShapeDtypeStruct((n_groups, 2, inner, t), F32)],
        compiler_params=_params("arbitrary"),
        name="a_sample_project",
    )(x, g.reshape(1, d), wq, wk_t, wv_t, cos_l, sin_l, cos_r, sin_r)


A_SAMPLE_HEADS = 4
A_SAMPLE_ROWS = 8


def _a_sample_kernel(n_q, dil, n_back, q_ref, new_ref, buf_ref, *refs):
    o_ref, lse_ref, st_ref = refs[-3:]
    b = pl.program_id(0)
    hpb = A_SAMPLE_HEADS
    width = hpb * HEAD_DIM
    buf_len = buf_ref.shape[-1]
    n_tok = new_ref.shape[-1]
    buf = buf_ref[0, 0].reshape(2 * width, buf_len)
    new = new_ref[0].reshape(2 * width, n_tok)
    new_sel = pltpu.roll(new, n_tok - n_q - b * n_q, 1)
    lane_w = lax.broadcasted_iota(jnp.int32, (2 * width, buf_len), 1)
    state = jnp.where(lane_w >= buf_len - n_q, jnp.tile(new_sel, (1, buf_len // n_tok)),
                      pltpu.roll(buf, buf_len - n_q, 1))
    st_ref[0, 0] = state.reshape(2, width, buf_len)

    rows = n_q * A_SAMPLE_ROWS
    own = (lax.broadcasted_iota(jnp.int32, (A_SAMPLE_ROWS, width), 1) // HEAD_DIM
           == lax.broadcasted_iota(jnp.int32, (A_SAMPLE_ROWS, width), 0))
    q = q_ref[0]
    qbd = jnp.concatenate(
        [jnp.where(own, jnp.broadcast_to(q[i:i + 1, :], (A_SAMPLE_ROWS, width)), 0.0) for i in range(n_q)],
        axis=0).astype(BF16)

    def masked(s, diff):
        s = jnp.where((diff & (dil - 1)) == 0, s, NEG)
        return jnp.where(lax.bitcast_convert_type(diff, jnp.uint32) <= jnp.uint32(n_back * dil), s, NEG)

    qi = lax.broadcasted_iota(jnp.int32, (rows, buf_len), 0) // A_SAMPLE_ROWS
    s_buf = masked(_dot(qbd, buf[:width].astype(BF16)),
                   buf_len + qi - lax.broadcasted_iota(jnp.int32, (rows, buf_len), 1))
    qi = lax.broadcasted_iota(jnp.int32, (rows, n_tok), 0) // A_SAMPLE_ROWS
    lane = lax.broadcasted_iota(jnp.int32, (rows, n_tok), 1)
    s_new = masked(_dot(qbd, new_sel[:width].astype(BF16)),
                   jnp.where(lane >= n_tok - n_q, qi - (lane - (n_tok - n_q)), -1))
    mx = jnp.maximum(jnp.max(s_buf, axis=-1, keepdims=True), jnp.max(s_new, axis=-1, keepdims=True))
    p_buf = jnp.exp(s_buf - mx)
    p_new = jnp.exp(s_new - mx)
    den = jnp.sum(p_buf, axis=-1, keepdims=True) + jnp.sum(p_new, axis=-1, keepdims=True)
    out = (_dot_nt(p_buf.astype(BF16), buf[width:].astype(BF16))
           + _dot_nt(p_new.astype(BF16), new_sel[width:].astype(BF16))) / den
    lse = jnp.broadcast_to(mx + jnp.log(den), (rows, width))
    for i in range(n_q):
        sl = slice(i * A_SAMPLE_ROWS, (i + 1) * A_SAMPLE_ROWS)
        o_ref[0, i:i + 1, :] = jnp.sum(jnp.where(own, out[sl], 0.0), axis=0, keepdims=True)
        lse_ref[0, i:i + 1, :] = jnp.sum(jnp.where(own, lse[sl], 0.0), axis=0, keepdims=True)


def _a_sample_group(q, kv_new, buf_t, prev, layer, n_layers, dec_batch, n_q, gi, n_groups, n_heads):
    window, dil = A_GROUPS[gi]
    n_back = window // dil
    inner = n_heads * HEAD_DIM
    buf_len = buf_t.shape[-1]
    n_tok = kv_new.shape[-1]
    width = A_SAMPLE_HEADS * HEAD_DIM
    nhb = inner // width
    assert buf_len == window and buf_len % n_tok == 0 and n_tok == dec_batch * n_q and dil & (dil - 1) == 0
    in_specs = [
        pl.BlockSpec((1, n_q, width), lambda b, hb: (b, 0, gi * nhb + hb)),
        pl.BlockSpec((1, 2, width, n_tok), lambda b, hb: (gi, 0, hb, 0)),
        pl.BlockSpec((1, 1, 2, width, buf_len), lambda b, hb: (layer, b, 0, hb, 0)),
    ]
    args = [q, kv_new, buf_t]
    aliases = {}
    if prev is not None:
        in_specs.append(pl.BlockSpec(memory_space=pl.ANY))
        args.append(prev)
        aliases = {len(args) - 1: 2}
    out_spec = pl.BlockSpec((1, n_q, width), lambda b, hb: (b, 0, hb))
    shape = jax.ShapeDtypeStruct((dec_batch, n_q, inner), F32)
    o, lse, st = pl.pallas_call(
        functools.partial(_a_sample_kernel, n_q, dil, n_back),
        grid=(dec_batch, nhb),
        in_specs=in_specs,
        out_specs=[out_spec, out_spec,
                   pl.BlockSpec((1, 1, 2, width, buf_len), lambda b, hb: (layer, b, 0, hb, 0))],
        out_shape=[shape, shape, jax.ShapeDtypeStruct((n_layers, dec_batch, 2, inner, buf_len), F32)],
        input_output_aliases=aliases,
        compiler_params=_params("parallel", "arbitrary"),
        name=f"a_sample_g{gi}",
    )(*args)
    return o.reshape(dec_batch * n_q, inner), lse.reshape(dec_batch * n_q, inner), st


def _mixer_a(xp, xs, g, w_in, w_out, bufs_t, prev_p, prev_s, layer, n_layers, batch, n, dec_batch, n_q, past_len):
    n_groups = len(A_GROUPS)
    d = xp.shape[1]
    inner = w_out.shape[0]
    n_heads = inner // HEAD_DIM
    w4 = w_in.reshape(d, n_groups, 3, inner)
    wq = jnp.transpose(w4[:, :, 0], (1, 0, 2)).astype(BF16)
    wk_t = jnp.transpose(w4[:, :, 1], (1, 2, 0)).astype(BF16)
    wv_t = jnp.transpose(w4[:, :, 2], (1, 2, 0)).astype(BF16)
    w_o = w_out.astype(BF16)

    cos_p, sin_p = _rope_tables(jnp.arange(n), HEAD_DIM, 1)
    cos_pl, sin_pl = jnp.tile(cos_p, (1, 2)), jnp.tile(sin_p, (1, 2))
    outs_p, lses_p, st_p = [], [], []
    for gi, (window, dil) in enumerate(A_GROUPS):
        cos_r = jnp.transpose(cos_p.reshape(n // dil, dil, HEAD_DIM), (1, 2, 0))
        sin_r = jnp.transpose(sin_p.reshape(n // dil, dil, HEAD_DIM), (1, 2, 0))
        o, lse = _a_prompt_group(xp, g, wq[gi], wk_t[gi], wv_t[gi], cos_pl, sin_pl, cos_r, sin_r,
                                 batch, n, gi, n_heads)
        outs_p.append(o)
        lses_p.append(lse)
        st_p.append(_a_state(xp, g, wk_t[gi], wv_t[gi], cos_p.T, sin_p.T,
                             None if prev_p is None else prev_p[gi], layer, n_layers, batch, n, window, n_heads))

    cos_s, sin_s = _rope_tables(past_len + jnp.arange(n_q), HEAD_DIM, 1)
    cos_s, sin_s = jnp.tile(cos_s, (dec_batch, 1)), jnp.tile(sin_s, (dec_batch, 1))
    q_s, kv_new = _a_sample_project(xs, g, wq, wk_t, wv_t, jnp.tile(cos_s, (1, 2)), jnp.tile(sin_s, (1, 2)),
                                    cos_s.T, sin_s.T, n_heads)
    q_s = q_s.reshape(dec_batch, n_q, n_groups * inner)
    outs_s, lses_s, st_s = [], [], []
    for gi in range(n_groups):
        o, lse, st = _a_sample_group(q_s, kv_new, bufs_t[gi], None if prev_s is None else prev_s[gi],
                                     layer, n_layers, dec_batch, n_q, gi, n_groups, n_heads)
        outs_s.append(o)
        lses_s.append(lse)
        st_s.append(st)
    xp = _a_combine(xp, outs_p, lses_p, w_o)
    xs = _a_combine(xs, outs_s, lses_s, w_o)
    return xp, xs, st_p, st_s


def _rms_matmul_kernel(x_ref, g_ref, w_ref, o_ref, h_ref):
    @pl.when(pl.program_id(1) == 0)
    def _():
        h_ref[...] = _rms(x_ref[...], g_ref[...]).astype(BF16)

    o_ref[...] = _dot(h_ref[...], w_ref[...])


def _rms_matmul(x, g, w):
    m, d = x.shape
    n_out = w.shape[1]
    tm = _tile(m, (512, 256, 128, 64, 32, 16, 8))
    tn = _tile(n_out, (640, 512, 256, 128))
    return pl.pallas_call(
        _rms_matmul_kernel,
        grid=(m // tm, n_out // tn),
        in_specs=[
            pl.BlockSpec((tm, d), lambda i, j: (i, 0)),
            pl.BlockSpec((1, d), lambda i, j: (0, 0)),
            pl.BlockSpec((d, tn), lambda i, j: (0, j)),
        ],
        out_specs=pl.BlockSpec((tm, tn), lambda i, j: (i, j)),
        out_shape=jax.ShapeDtypeStruct((m, n_out), F32),
        scratch_shapes=[pltpu.VMEM((tm, d), BF16)],
        compiler_params=_params("parallel", "arbitrary"),
        name="rms_matmul",
    )(x, g.reshape(1, d), w)


def _flash_kernel(n_stack, dk, dv, v_from_k, tk, *refs):
    if v_from_k:
        q_ref, k_ref, o_ref, m_scr, l_scr, acc_scr = refs
        v_ref = k_ref
    else:
        q_ref, k_ref, v_ref, o_ref, m_scr, l_scr, acc_scr = refs
    tq = q_ref.shape[1]
    rows = n_stack * tq
    i = pl.program_id(2)
    qb = q_ref[0]
    q = qb if n_stack == 1 else jnp.concatenate([qb[:, s * dk:(s + 1) * dk] for s in range(n_stack)], axis=0)
    m_scr[...] = jnp.full(m_scr.shape, NEG, F32)
    l_scr[...] = jnp.zeros(l_scr.shape, F32)
    acc_scr[...] = jnp.zeros(acc_scr.shape, F32)

    def step(kb, masked):
        k0 = pl.multiple_of(kb * tk, tk)
        kblk = k_ref[0, pl.ds(k0, tk), :]
        vblk = kblk[:, :dv] if v_from_k else v_ref[0, pl.ds(k0, tk), :]
        s = _dot_nt(q, kblk)
        if masked:
            qpos = i * tq + lax.broadcasted_iota(jnp.int32, (rows, tk), 0) % tq
            col = lax.broadcasted_iota(jnp.int32, (rows, tk), 1)
            s = jnp.where(col + k0 <= qpos, s, NEG)
        m_old = m_scr[...]
        m_new = jnp.maximum(m_old, jnp.max(s, axis=-1, keepdims=True))
        alpha = jnp.exp(m_old - m_new)
        p = jnp.exp(s - jnp.tile(m_new, (1, tk // LANES)))
        l_scr[...] = alpha * l_scr[...] + jnp.sum(p, axis=-1, keepdims=True)
        acc_scr[...] = jnp.tile(alpha, (1, dv // LANES)) * acc_scr[...] + _dot(p.astype(BF16), vblk)
        m_scr[...] = m_new

    def full_body(kb, carry):
        step(kb, False)
        return carry

    def diag_body(kb, carry):
        step(kb, True)
        return carry

    n_full = lax.div(i * tq + 1, tk)
    n_total = lax.div((i + 1) * tq + tk - 1, tk)
    lax.fori_loop(0, n_full, full_body, 0)
    lax.fori_loop(n_full, n_total, diag_body, 0)
    out = acc_scr[...] / jnp.tile(l_scr[...], (1, dv // LANES))
    for s in range(n_stack):
        o_ref[0, :, s * dv:(s + 1) * dv] = out[s * tq:(s + 1) * tq, :]


def _flash(q, k, v, n_kv_heads, n_stack, dk, dv, tq, tk):
    batch, n, _ = q.shape
    tq = min(tq, n)
    tk = min(tk, n)
    assert n % tq == 0 and n % tk == 0 and tk % LANES == 0 and dv % LANES == 0
    v_from_k = v is None
    in_specs = [
        pl.BlockSpec((1, tq, n_stack * dk), lambda b, h, i: (b, i, h)),
        pl.BlockSpec((1, n, dk), lambda b, h, i: (b, 0, h)),
    ]
    args = [q, k]
    if not v_from_k:
        in_specs.append(pl.BlockSpec((1, n, dv), lambda b, h, i: (b, 0, h)))
        args.append(v)
    rows = n_stack * tq
    return pl.pallas_call(
        functools.partial(_flash_kernel, n_stack, dk, dv, v_from_k, tk),
        grid=(batch, n_kv_heads, n // tq),
        in_specs=in_specs,
        out_specs=pl.BlockSpec((1, tq, n_stack * dv), lambda b, h, i: (b, i, h)),
        out_shape=jax.ShapeDtypeStruct((batch, n, n_kv_heads * n_stack * dv), F32),
        scratch_shapes=[pltpu.VMEM((rows, LANES), F32), pltpu.VMEM((rows, LANES), F32),
                        pltpu.VMEM((rows, dv), F32)],
        compiler_params=_params("parallel", "parallel", "arbitrary"),
        name="flash",
    )(*args)


B_PAGES_PER_STEP = 8
C_PAGES_PER_STEP = 16


def _log_sigmoid(z):
    return jnp.minimum(z, 0.0) - jnp.log1p(jnp.exp(-jnp.abs(z)))


def _split3(c):
    hi = c.astype(BF16)
    r = c - hi.astype(F32)
    mid = r.astype(BF16)
    lo = (r - mid.astype(F32)).astype(BF16)
    return hi, mid, lo


def _pad_heads(y):
    t, w = y.shape
    low = lax.broadcasted_iota(jnp.int32, (t, LANES), 1) < HEAD_DIM
    pieces = []
    for p in range(w // LANES):
        blk = y[:, p * LANES:(p + 1) * LANES]
        pieces.append(jnp.where(low, blk, 0.0))
        pieces.append(jnp.where(low, pltpu.roll(blk, HEAD_DIM, 1), 0.0))
    return jnp.concatenate(pieces, axis=1)


def _b_project_kernel(scale, n_heads, x_ref, g_ref, w_ref, wf_ref, bias_ref, pq_ref, pk_ref, oq_ref, ok_ref,
                      kv_ref, lf_ref, qa_ref, ka_ref, va_ref, carry_ref):
    @pl.when(pl.program_id(1) == 0)
    def _():
        carry_ref[...] = jnp.zeros_like(carry_ref)

    inner = n_heads * HEAD_DIM
    h = _rms(x_ref[...], g_ref[...]).astype(BF16)
    y = _dot(h, w_ref[...])
    z = _dot(h, wf_ref[...]) + bias_ref[...]
    tm = z.shape[0]
    lane = lax.broadcasted_iota(jnp.int32, (tm, LANES), 1)
    lf = jnp.where(lane < n_heads, _log_sigmoid(z), 0.0)
    lf_ref[...] = lf[:, :n_heads]
    tri = (lax.broadcasted_iota(jnp.int32, (tm, tm), 0) >= lax.broadcasted_iota(jnp.int32, (tm, tm), 1))
    tri = jnp.where(tri, 1.0, 0.0).astype(BF16)
    hi, mid, lo = _split3(lf)
    c = _dot(tri, hi) + _dot(tri, mid) + _dot(tri, lo) + carry_ref[...]
    carry_ref[...] = c[tm - 1:tm, :]
    c3 = jnp.concatenate(_split3(c), axis=1)
    q_aug = _dot(c3, pq_ref[...]) + oq_ref[...]
    k_aug = _dot(c3, pk_ref[...]) + ok_ref[...]
    q = y[:, :inner] * scale
    k = y[:, inner:2 * inner]
    v = y[:, 2 * inner:]
    kv_ref[:, :inner] = k
    kv_ref[:, inner:] = v
    qa_ref[...] = (_pad_heads(q) + q_aug).astype(BF16)
    ka_ref[...] = (_pad_heads(k) + k_aug).astype(BF16)
    va_ref[...] = _pad_heads(v).astype(BF16)


def _b_aug_constants(n_heads):
    pq = jnp.zeros((3 * LANES, n_heads * LANES), F32)
    pk = jnp.zeros((3 * LANES, n_heads * LANES), F32)
    oq = jnp.zeros((1, n_heads * LANES), F32)
    ok = jnp.zeros((1, n_heads * LANES), F32)
    heads = jnp.arange(n_heads)
    for piece in range(3):
        pq = pq.at[piece * LANES + heads, heads * LANES + HEAD_DIM + piece].set(1.0)
        pk = pk.at[piece * LANES + heads, heads * LANES + HEAD_DIM + 3 + piece].set(-1.0)
        ok = ok.at[0, heads * LANES + HEAD_DIM + piece].set(1.0)
        oq = oq.at[0, heads * LANES + HEAD_DIM + 3 + piece].set(1.0)
    return pq.astype(BF16), pk.astype(BF16), oq, ok


def _b_project(x, g, w_qkv, w_f, bias, batch, n, n_heads):
    m, d = x.shape
    inner = n_heads * HEAD_DIM
    tm = _tile(n, (256, 128, 64, 32, 16, 8))
    nt = n // tm
    pq, pk, oq, ok = _b_aug_constants(n_heads)
    row = lambda b, t: (b * nt + t, 0)
    fix = lambda b, t: (0, 0)
    wide = n_heads * LANES
    return pl.pallas_call(
        functools.partial(_b_project_kernel, HEAD_DIM ** -0.5, n_heads),
        grid=(batch, nt),
        in_specs=[
            pl.BlockSpec((tm, d), row),
            pl.BlockSpec((1, d), fix),
            pl.BlockSpec((d, 3 * inner), fix),
            pl.BlockSpec((d, LANES), fix),
            pl.BlockSpec((1, LANES), fix),
            pl.BlockSpec((3 * LANES, wide), fix),
            pl.BlockSpec((3 * LANES, wide), fix),
            pl.BlockSpec((1, wide), fix),
            pl.BlockSpec((1, wide), fix),
        ],
        out_specs=[
            pl.BlockSpec((tm, 2 * inner), row),
            pl.BlockSpec((tm, n_heads), row),
            pl.BlockSpec((tm, wide), row),
            pl.BlockSpec((tm, wide), row),
            pl.BlockSpec((tm, wide), row),
        ],
        out_shape=[
            jax.ShapeDtypeStruct((m, 2 * inner), F32),
            jax.ShapeDtypeStruct((m, n_heads), F32),
            jax.ShapeDtypeStruct((m, wide), BF16),
            jax.ShapeDtypeStruct((m, wide), BF16),
            jax.ShapeDtypeStruct((m, wide), BF16),
        ],
        scratch_shapes=[pltpu.VMEM((1, LANES), F32)],
        compiler_params=_params("parallel", "arbitrary"),
        name="b_project",
    )(x, g.reshape(1, d), w_qkv, w_f, bias, pq, pk, oq, ok)


def _online_update(s, m_scr, l_scr, acc_scr, pv):
    m_old = m_scr[...]
    m_new = jnp.maximum(m_old, jnp.max(s, axis=-1, keepdims=True))
    alpha = jnp.exp(m_old - m_new)
    p = jnp.exp(s - jnp.tile(m_new, (1, s.shape[1] // LANES)))
    l_scr[...] = alpha * l_scr[...] + jnp.sum(p, axis=-1, keepdims=True)
    acc_scr[...] = jnp.tile(alpha, (1, acc_scr.shape[1] // LANES)) * acc_scr[...] + pv(p.astype(BF16))
    m_scr[...] = m_new


def _init_softmax(s, m_scr, l_scr, acc_scr, pv):
    mx = jnp.max(s, axis=-1, keepdims=True)
    pr = jnp.exp(s - mx)
    m_scr[...] = jnp.broadcast_to(mx, m_scr.shape)
    l_scr[...] = jnp.broadcast_to(jnp.sum(pr, axis=-1, keepdims=True), l_scr.shape)
    acc_scr[...] = pv(pr.astype(BF16))


def _b_sample_kernel(scale, n_q, n_heads, n_group, pt_ref, y_ref, bias_ref, *refs):
    kt_refs = refs[:n_group]
    vt_refs = refs[n_group:2 * n_group]
    lft_refs = refs[2 * n_group:3 * n_group]
    (o_ref, lfo_ref, qbd_scr, at_scr, m_scr, l_scr, acc_scr, tail_scr, kn_scr, vn_scr) = refs[3 * n_group:]
    p = pl.program_id(1)
    inner = n_heads * HEAD_DIM
    rows = n_q * n_heads
    slots = kt_refs[0].shape[3]

    @pl.when(p == 0)
    def _():
        y = y_ref[0]
        q = y[:, :inner] * scale
        z = y[:, 3 * inner:] + bias_ref[...]
        lane = lax.broadcasted_iota(jnp.int32, (n_q, LANES), 1)
        lf = jnp.where(lane < n_heads, _log_sigmoid(z), 0.0)
        lfo_ref[0] = lf
        run = [lf[0:1]]
        for i in range(1, n_q):
            run.append(run[-1] + lf[i:i + 1])
        eye = (lax.broadcasted_iota(jnp.int32, (n_heads, LANES), 0)
               == lax.broadcasted_iota(jnp.int32, (n_heads, LANES), 1))
        cols = [jnp.sum(jnp.where(eye, jnp.broadcast_to(r, (n_heads, LANES)), 0.0), axis=1, keepdims=True)
                for r in run]
        a_col = jnp.concatenate(cols, axis=0)
        at_scr[...] = jnp.broadcast_to(a_col, at_scr.shape)
        lane_head = lax.broadcasted_iota(jnp.int32, (n_heads, inner), 1) // HEAD_DIM
        own = lane_head == lax.broadcasted_iota(jnp.int32, (n_heads, inner), 0)
        qbd = jnp.concatenate(
            [jnp.where(own, jnp.broadcast_to(q[i:i + 1, :], (n_heads, inner)), 0.0) for i in range(n_q)],
            axis=0).astype(BF16)
        qbd_scr[...] = qbd
        kn_scr[...] = jnp.zeros_like(kn_scr)
        vn_scr[...] = jnp.zeros_like(vn_scr)
        kn_scr[0:n_q, :] = y[:, inner:2 * inner]
        vn_scr[0:n_q, :] = y[:, 2 * inner:3 * inner]
        s = _dot_nt(qbd, kn_scr[...].astype(BF16))
        col = lax.broadcasted_iota(jnp.int32, (rows, LANES), 1)
        a_key = jnp.zeros((rows, LANES), F32)
        for i in range(n_q):
            a_key = jnp.where(col == i, jnp.concatenate([cols[i]] * n_q, axis=0), a_key)
        s = s + a_col - a_key
        qi = lax.broadcasted_iota(jnp.int32, (rows, LANES), 0) // n_heads
        s = jnp.where(col <= qi, s, NEG)
        vn = vn_scr[...].astype(BF16)
        _init_softmax(s, m_scr, l_scr, acc_scr, lambda pb: _dot(pb, vn))
        tail_scr[...] = jnp.zeros_like(tail_scr)

    lfts = [r[0] for r in lft_refs]
    later = (lax.broadcasted_iota(jnp.int32, (slots, slots), 0) > lax.broadcasted_iota(jnp.int32, (slots, slots), 1))
    later = jnp.where(later, 1.0, 0.0).astype(BF16)
    pieces = [piece for lft in lfts for piece in _split3(lft)]
    r3 = _dot(jnp.concatenate(pieces, axis=0), later)
    qbd = qbd_scr[...]
    tail = tail_scr[...]
    parts = []
    for gi in range(n_group):
        base = 3 * n_heads * gi
        r = (r3[base:base + n_heads] + r3[base + n_heads:base + 2 * n_heads]
             + r3[base + 2 * n_heads:base + 3 * n_heads] + tail)
        parts.append(_dot(qbd, kt_refs[gi][0, 0].astype(BF16)) + jnp.concatenate([r] * n_q, axis=0))
        tail = tail + jnp.sum(lfts[gi], axis=1, keepdims=True)
    tail_scr[...] = tail
    s = jnp.concatenate(parts, axis=1) + jnp.tile(at_scr[...], (1, n_group * slots // LANES))

    def pv(pb):
        out = _dot_nt(pb[:, :slots], vt_refs[0][0, 0].astype(BF16))
        for gi in range(1, n_group):
            out = out + _dot_nt(pb[:, gi * slots:(gi + 1) * slots], vt_refs[gi][0, 0].astype(BF16))
        return out

    _online_update(s, m_scr, l_scr, acc_scr, pv)

    @pl.when(p == pl.num_programs(1) - 1)
    def _():
        lane_head = lax.broadcasted_iota(jnp.int32, (n_heads, inner), 1) // HEAD_DIM
        own = lane_head == lax.broadcasted_iota(jnp.int32, (n_heads, inner), 0)
        out = acc_scr[...] / jnp.tile(l_scr[...], (1, inner // LANES))
        for i in range(n_q):
            o_ref[0, i:i + 1, :] = jnp.sum(jnp.where(own, out[i * n_heads:(i + 1) * n_heads], 0.0),
                                           axis=0, keepdims=True)


def _b_sample(y, bias, cache_kv, cache_logf, page_table, dec_batch, n_q, n_heads, n_group):
    inner = n_heads * HEAD_DIM
    pool, slots = cache_kv.shape[0], cache_kv.shape[1]
    n_pages = page_table.shape[1]
    n_group = min(n_group, n_pages)
    assert n_pages % n_group == 0 and slots % LANES == 0
    kvt = jnp.transpose(cache_kv, (0, 2, 3, 4, 1)).reshape(pool, 2, inner, slots)
    lft = jnp.transpose(cache_logf, (0, 2, 1))
    yv = y.reshape(dec_batch, n_q, 3 * inner + LANES)
    rows = n_q * n_heads

    def page(b, p, pt, gi):
        return pt[b, n_pages - 1 - (p * n_group + gi)]

    in_specs = [
        pl.BlockSpec((1, n_q, 3 * inner + LANES), lambda b, p, pt: (b, 0, 0)),
        pl.BlockSpec((1, LANES), lambda b, p, pt: (0, 0)),
    ]
    for kind in (0, 1):
        for gi in range(n_group):
            in_specs.append(pl.BlockSpec((1, 1, inner, slots),
                                         lambda b, p, pt, gi=gi, kind=kind: (page(b, p, pt, gi), kind, 0, 0)))
    for gi in range(n_group):
        in_specs.append(pl.BlockSpec((1, n_heads, slots), lambda b, p, pt, gi=gi: (page(b, p, pt, gi), 0, 0)))
    grid_spec = pltpu.PrefetchScalarGridSpec(
        num_scalar_prefetch=1,
        grid=(dec_batch, n_pages // n_group),
        in_specs=in_specs,
        out_specs=[
            pl.BlockSpec((1, n_q, inner), lambda b, p, pt: (b, 0, 0)),
            pl.BlockSpec((1, n_q, LANES), lambda b, p, pt: (b, 0, 0)),
        ],
        scratch_shapes=[
            pltpu.VMEM((rows, inner), BF16),
            pltpu.VMEM((rows, LANES), F32),
            pltpu.VMEM((rows, LANES), F32),
            pltpu.VMEM((rows, LANES), F32),
            pltpu.VMEM((rows, inner), F32),
            pltpu.VMEM((n_heads, 1), F32),
            pltpu.VMEM((LANES, inner), F32),
            pltpu.VMEM((LANES, inner), F32),
        ],
    )
    o, lf = pl.pallas_call(
        functools.partial(_b_sample_kernel, HEAD_DIM ** -0.5, n_q, n_heads, n_group),
        grid_spec=grid_spec,
        out_shape=[jax.ShapeDtypeStruct((dec_batch, n_q, inner), F32),
                   jax.ShapeDtypeStruct((dec_batch, n_q, LANES), F32)],
        compiler_params=_params("parallel", "arbitrary"),
        name="b_sample",
    )(page_table, yv, bias, *([kvt] * (2 * n_group)), *([lft] * n_group))
    return o.reshape(dec_batch * n_q, inner), lf[:, :, :n_heads]


def _mixer_b(xp, xs, g, w_in, f_bias, w_out, cache_kv, cache_logf, page_table, batch, n, dec_batch, n_q):
    inner = w_out.shape[0]
    n_heads = inner // HEAD_DIM
    d = xp.shape[1]
    w_qkv = w_in[:, :3 * inner].astype(BF16)
    w_f = jnp.pad(w_in[:, 3 * inner:], ((0, 0), (0, LANES - n_heads))).astype(BF16)
    bias = jnp.pad(f_bias.reshape(1, n_heads), ((0, 0), (0, LANES - n_heads)))
    w_out_pad = jnp.pad(w_out.reshape(n_heads, HEAD_DIM, d),
                        ((0, 0), (0, LANES - HEAD_DIM), (0, 0))).reshape(n_heads * LANES, d).astype(BF16)

    kv_p, lf_p, qa, ka, va = _b_project(xp, g, w_qkv, w_f, bias, batch, n, n_heads)
    wide = n_heads * LANES
    o_p = _flash(qa.reshape(batch, n, wide), ka.reshape(batch, n, wide), va.reshape(batch, n, wide),
                 n_heads, 1, LANES, LANES, 512, 512)
    xp = _outproj(xp, o_p.reshape(batch * n, wide), w_out_pad)

    y_s = _rms_matmul(xs, g, jnp.concatenate([w_qkv, w_f], axis=1))
    o_s, lf_s = _b_sample(y_s, bias, cache_kv, cache_logf, page_table, dec_batch, n_q, n_heads, B_PAGES_PER_STEP)
    xs = _outproj(xs, o_s, w_out.astype(BF16))
    kv_s = y_s[:, inner:3 * inner].reshape(dec_batch, n_q, 2, n_heads, HEAD_DIM)
    return (xp, xs, kv_p.reshape(batch, n, 2, n_heads, HEAD_DIM), lf_p.reshape(batch, n, n_heads), kv_s, lf_s)


def _c_project_kernel(scale, n_heads, x_ref, g_ref, wq_ref, wkv_ref, wpe_ref, qn_ref, kvn_ref, wqn_ref, wqr_ref,
                      wuk_ref, cos_ref, sin_ref, st_ref, kc_ref, qa_ref):
    h = _rms(x_ref[...], g_ref[...]).astype(BF16)
    cq = _rms(_dot(h, wq_ref[...]), qn_ref[...]).astype(BF16)
    ckv = _rms(_dot(h, wkv_ref[...]), kvn_ref[...])
    cos, sin = cos_ref[...], sin_ref[...]
    ype = _dot(h, wpe_ref[...])
    kpe = ype * cos + _swap_halves(ype, C_ROPE // 2) * sin
    lat = ckv.shape[1]
    st_ref[:, :lat] = ckv
    st_ref[:, lat:lat + C_ROPE] = kpe[:, :C_ROPE]
    kc_ref[:, :lat] = ckv.astype(BF16)
    kc_ref[:, lat:] = kpe.astype(BF16)
    qn = _dot(cq, wqn_ref[...]).astype(BF16)
    qr = _dot(cq, wqr_ref[...])
    dq = lat + LANES
    for pair in range(n_heads // 2):
        ql = _dot(qn[:, pair * LANES:(pair + 1) * LANES], wuk_ref[pair])
        for j in range(2):
            head = 2 * pair + j
            r = qr[:, head * LANES:(head + 1) * LANES]
            r = r * cos + _swap_halves(r, C_ROPE // 2) * sin
            qa_ref[:, head * dq:head * dq + lat] = (ql[:, j * lat:(j + 1) * lat] * scale).astype(BF16)
            qa_ref[:, head * dq + lat:(head + 1) * dq] = (r * scale).astype(BF16)


def _c_project(x, g, w, q_norm, kv_norm, cos_t, sin_t, n_heads):
    m, d = x.shape
    lat = w["kv"].shape[1]
    q_lora = w["q"].shape[1]
    p = cos_t.shape[0]
    tm = _tile(p, (256, 128, 64, 32, 16, 8))
    npb = p // tm
    dq = lat + LANES
    row = lambda i: (i, 0)
    fix = lambda i: (0, 0)
    tab = lambda i: (i % npb, 0)
    return pl.pallas_call(
        functools.partial(_c_project_kernel, (C_NOPE + C_ROPE) ** -0.5, n_heads),
        grid=(m // tm,),
        in_specs=[
            pl.BlockSpec((tm, d), row),
            pl.BlockSpec((1, d), fix),
            pl.BlockSpec((d, q_lora), fix),
            pl.BlockSpec((d, lat), fix),
            pl.BlockSpec((d, LANES), fix),
            pl.BlockSpec((1, q_lora), fix),
            pl.BlockSpec((1, lat), fix),
            pl.BlockSpec((q_lora, n_heads * C_NOPE), fix),
            pl.BlockSpec((q_lora, n_heads * LANES), fix),
            pl.BlockSpec((n_heads // 2, LANES, 2 * lat), lambda i: (0, 0, 0)),
            pl.BlockSpec((tm, LANES), tab),
            pl.BlockSpec((tm, LANES), tab),
        ],
        out_specs=[
            pl.BlockSpec((tm, lat + C_ROPE), row),
            pl.BlockSpec((tm, dq), row),
            pl.BlockSpec((tm, n_heads * dq), row),
        ],
        out_shape=[
            jax.ShapeDtypeStruct((m, lat + C_ROPE), F32),
            jax.ShapeDtypeStruct((m, dq), BF16),
            jax.ShapeDtypeStruct((m, n_heads * dq), BF16),
        ],
        compiler_params=_params("parallel"),
        name="c_project",
    )(x, g.reshape(1, d), w["q"], w["kv"], w["pe"], q_norm.reshape(1, -1), kv_norm.reshape(1, -1),
      w["qn"], w["qr"], w["uk"], cos_t, sin_t)


def _c_sample_kernel(n_q, n_heads, lat, n_group, pt_ref, q_ref, kn_ref, *refs):
    page_refs = refs[:n_group]
    o_ref, m_scr, l_scr, acc_scr, kn_scr, kr_scr = refs[n_group:]
    p = pl.program_id(1)
    rows = n_q * n_heads
    slots = page_refs[0].shape[2]
    q = q_ref[0]

    @pl.when(p == 0)
    def _():
        kn_scr[...] = jnp.zeros_like(kn_scr)
        kr_scr[...] = jnp.zeros_like(kr_scr)
        kn_scr[0:n_q, :] = kn_ref[0].astype(F32)
        knew = kn_scr[...].astype(BF16)
        s = _dot_nt(q, knew)
        col = lax.broadcasted_iota(jnp.int32, (rows, LANES), 1)
        qi = lax.broadcasted_iota(jnp.int32, (rows, LANES), 0) // n_heads
        s = jnp.where(col <= qi, s, NEG)
        _init_softmax(s, m_scr, l_scr, acc_scr, lambda pb: _dot(pb, knew[:, :lat]))

    pages = [r[0] for r in page_refs]
    lat_t = jnp.concatenate([pg[:lat] for pg in pages], axis=1).astype(BF16)
    for gi in range(n_group):
        kr_scr[0:C_ROPE, gi * slots:(gi + 1) * slots] = pages[gi][lat:lat + C_ROPE]
    s = _dot(q[:, :lat], lat_t) + _dot(q[:, lat:], kr_scr[...].astype(BF16))
    _online_update(s, m_scr, l_scr, acc_scr, lambda pb: _dot_nt(pb, lat_t))

    @pl.when(p == pl.num_programs(1) - 1)
    def _():
        o_ref[0] = acc_scr[...] / jnp.tile(l_scr[...], (1, lat // LANES))


def _c_sample(q, kn, cache, page_table, dec_batch, n_q, n_heads, lat, n_group):
    pool, slots, width = cache.shape
    n_pages = page_table.shape[1]
    n_group = min(n_group, n_pages)
    assert n_pages % n_group == 0 and slots % LANES == 0 and lat % LANES == 0
    cache_t = jnp.transpose(cache, (0, 2, 1))
    rows = n_q * n_heads
    dq = lat + LANES
    in_specs = [
        pl.BlockSpec((1, rows, dq), lambda b, p, pt: (b, 0, 0)),
        pl.BlockSpec((1, n_q, dq), lambda b, p, pt: (b, 0, 0)),
    ]
    for gi in range(n_group):
        in_specs.append(pl.BlockSpec((1, width, slots), lambda b, p, pt, gi=gi: (pt[b, p * n_group + gi], 0, 0)))
    grid_spec = pltpu.PrefetchScalarGridSpec(
        num_scalar_prefetch=1,
        grid=(dec_batch, n_pages // n_group),
        in_specs=in_specs,
        out_specs=pl.BlockSpec((1, rows, lat), lambda b, p, pt: (b, 0, 0)),
        scratch_shapes=[
            pltpu.VMEM((rows, LANES), F32),
            pltpu.VMEM((rows, LANES), F32),
            pltpu.VMEM((rows, lat), F32),
            pltpu.VMEM((LANES, dq), F32),
            pltpu.VMEM((LANES, n_group * slots), F32),
        ],
    )
    return pl.pallas_call(
        functools.partial(_c_sample_kernel, n_q, n_heads, lat, n_group),
        grid_spec=grid_spec,
        out_shape=jax.ShapeDtypeStruct((dec_batch, rows, lat), F32),
        compiler_params=_params("parallel", "arbitrary"),
        name="c_sample",
    )(page_table, q, kn, *([cache_t] * n_group))


def _c_up_kernel(n_pairs, x_ref, ol_ref, wuv_ref, wo_ref, y_ref):
    ol = ol_ref[...]
    w = ol.shape[1] // n_pairs
    o = jnp.concatenate([_dot(ol[:, p * w:(p + 1) * w].astype(BF16), wuv_ref[p]) for p in range(n_pairs)], axis=1)
    y_ref[...] = x_ref[...] + _dot(o.astype(BF16), wo_ref[...])


def _c_up(x, o_lat, w_uv, w_out):
    m, d = x.shape
    k = o_lat.shape[1]
    tm = _tile(m, (256, 128, 64, 32, 16, 8))
    return pl.pallas_call(
        functools.partial(_c_up_kernel, w_uv.shape[0]),
        grid=(m // tm,),
        in_specs=[
            pl.BlockSpec((tm, d), lambda i: (i, 0)),
            pl.BlockSpec((tm, k), lambda i: (i, 0)),
            pl.BlockSpec(w_uv.shape, lambda i: (0, 0, 0)),
            pl.BlockSpec(w_out.shape, lambda i: (0, 0)),
        ],
        out_specs=pl.BlockSpec((tm, d), lambda i: (i, 0)),
        out_shape=jax.ShapeDtypeStruct((m, d), F32),
        compiler_params=_params("parallel"),
        name="c_up",
    )(x, o_lat, w_uv, w_out)


def _mixer_c(xp, xs, g, w_in, q_norm, kv_norm, w_qb, w_kvb, w_out, cache, page_table,
             batch, n, dec_batch, n_q, past_len):
    q_lora, lat = q_norm.shape[0], kv_norm.shape[0]
    n_heads = w_out.shape[0] // C_V
    dq = lat + LANES
    qb3 = w_qb.reshape(q_lora, n_heads, C_NOPE + C_ROPE)
    kvb3 = w_kvb.reshape(lat, n_heads, C_NOPE + C_V)
    uk_t = jnp.transpose(kvb3[..., :C_NOPE], (1, 2, 0))
    uk = jnp.zeros((n_heads // 2, LANES, 2 * lat), F32)
    uk = uk.at[:, :C_NOPE, :lat].set(uk_t[0::2]).at[:, C_NOPE:, lat:].set(uk_t[1::2])
    uv_h = jnp.transpose(kvb3[..., C_NOPE:], (1, 0, 2))
    uv = jnp.zeros((n_heads // 2, 2 * lat, LANES), F32)
    uv = uv.at[:, :lat, :C_V].set(uv_h[0::2]).at[:, lat:, C_V:].set(uv_h[1::2])
    w = {
        "q": w_in[:, :q_lora].astype(BF16),
        "kv": w_in[:, q_lora:q_lora + lat].astype(BF16),
        "pe": jnp.pad(w_in[:, q_lora + lat:], ((0, 0), (0, LANES - C_ROPE))).astype(BF16),
        "qn": qb3[..., :C_NOPE].reshape(q_lora, n_heads * C_NOPE).astype(BF16),
        "qr": jnp.pad(qb3[..., C_NOPE:], ((0, 0), (0, 0), (0, LANES - C_ROPE))).reshape(q_lora, n_heads * LANES).astype(BF16),
        "uk": uk.astype(BF16),
    }
    uv = uv.astype(BF16)
    w_o = w_out.astype(BF16)

    cos_p, sin_p = _rope_tables(jnp.arange(n), C_ROPE, 1, pad_to=LANES)
    st_p, kc_p, qa_p = _c_project(xp, g, w, q_norm, kv_norm, cos_p, sin_p, n_heads)
    o_lat_p = _flash(qa_p.reshape(batch, n, n_heads * dq), kc_p.reshape(batch, n, dq), None,
                     1, n_heads, dq, lat, 128, 512)
    xp = _c_up(xp, o_lat_p.reshape(batch * n, n_heads * lat), uv, w_o)

    cos_s, sin_s = _rope_tables(past_len + jnp.arange(n_q), C_ROPE, 1, pad_to=LANES)
    cos_s, sin_s = jnp.tile(cos_s, (dec_batch, 1)), jnp.tile(sin_s, (dec_batch, 1))
    st_s, kc_s, qa_s = _c_project(xs, g, w, q_norm, kv_norm, cos_s, sin_s, n_heads)
    o_lat_s = _c_sample(qa_s.reshape(dec_batch, n_q * n_heads, dq), kc_s.reshape(dec_batch, n_q, dq),
                        cache, page_table, dec_batch, n_q, n_heads, lat, C_PAGES_PER_STEP)
    o_lat_s = o_lat_s.reshape(dec_batch * n_q, n_heads * lat)
    xs = _c_up(xs, o_lat_s, uv, w_o)
    return xp, xs, st_p.reshape(batch, n, lat + C_ROPE), st_s.reshape(dec_batch, n_q, lat + C_ROPE)


def kernel(x_prompt, x_sample, cache_a_kv0, cache_a_kv1, cache_a_kv2, cache_b_kv, cache_b_logf, cache_c_ckv, page_table, p_prompt, p_sample, norm_g, final_g, ffn_w_in, ffn_w_out, ple_w_gate, ple_w_proj, a_w_in, a_w_out, b_w_in, b_f_bias, b_w_out, c_w_in, c_q_norm, c_kv_norm, c_w_qb, c_w_kvb, c_w_out):
    batch, n, d = x_prompt.shape
    dec_batch, n_q, _ = x_sample.shape
    depth = norm_g.shape[0]
    past_len = page_table.shape[1] * cache_b_kv.shape[2]
    xp = x_prompt.reshape(batch * n, d)
    xs = x_sample.reshape(dec_batch * n_q, d)
    n_a_layers = cache_a_kv0.shape[0]
    a_bufs = [jnp.transpose(buf, (0, 1, 3, 4, 5, 2)).reshape(buf.shape[0], buf.shape[1], 2, -1, buf.shape[2])
              for buf in (cache_a_kv0, cache_a_kv1, cache_a_kv2)]
    a_st_p, a_st_s = None, None
    b_kv_p, b_lf_p, b_kv_s, b_lf_s, c_p, c_s = [], [], [], [], [], []
    for i in range(depth):
        kind, j = i % N_MIXERS, i // N_MIXERS
        w1, w2 = ffn_w_in[i, 0].astype(BF16), ffn_w_out[i, 0].astype(BF16)
        xp = _ffn(xp, norm_g[i, 0], w1, w2)
        xs = _ffn(xs, norm_g[i, 0], w1, w2)
        if kind == 0:
            xp, xs, a_st_p, a_st_s = _mixer_a(xp, xs, norm_g[i, 1], a_w_in[j], a_w_out[j], a_bufs, a_st_p, a_st_s,
                                              j, n_a_layers, batch, n, dec_batch, n_q, past_len)
        elif kind == 1:
            xp, xs, kvp, lfp, kvs, lfs = _mixer_b(xp, xs, norm_g[i, 1], b_w_in[j], b_f_bias[j], b_w_out[j],
                                                  cache_b_kv[j], cache_b_logf[j], page_table,
                                                  batch, n, dec_batch, n_q)
            b_kv_p.append(kvp)
            b_lf_p.append(lfp)
            b_kv_s.append(kvs)
            b_lf_s.append(lfs)
        else:
            xp, xs, cp, cs = _mixer_c(xp, xs, norm_g[i, 1], c_w_in[j], c_q_norm[j], c_kv_norm[j], c_w_qb[j],
                                      c_w_kvb[j], c_w_out[j], cache_c_ckv[j], page_table,
                                      batch, n, dec_batch, n_q, past_len)
            c_p.append(cp)
            c_s.append(cs)
        w1, w2 = ffn_w_in[i, 1].astype(BF16), ffn_w_out[i, 1].astype(BF16)
        xp = _ffn(xp, norm_g[i, 2], w1, w2)
        xs = _ffn(xs, norm_g[i, 2], w1, w2)
        wg, wp = ple_w_gate[i].astype(BF16), ple_w_proj[i].astype(BF16)
        fg = final_g if i == depth - 1 else None
        xp = _ple(xp, p_prompt[i].reshape(batch * n, -1), norm_g[i, 3], wg, wp, fg)
        xs = _ple(xs, p_sample[i].reshape(dec_batch * n_q, -1), norm_g[i, 3], wg, wp, fg)
    def a_state(st):
        layers, rows, _, _, width = st.shape
        return jnp.transpose(st.reshape(layers, rows, 2, -1, HEAD_DIM, width), (0, 1, 5, 2, 3, 4))

    return (xp.reshape(batch, n, d), xs.reshape(dec_batch, n_q, d),
            a_state(a_st_p[0]), a_state(a_st_p[1]), a_state(a_st_p[2]),
            jnp.stack(b_kv_p), jnp.stack(b_lf_p), jnp.stack(c_p),
            a_state(a_st_s[0]), a_state(a_st_s[1]), a_state(a_st_s[2]),
            jnp.stack(b_kv_s), jnp.stack(b_lf_s), jnp.stack(c_s))
```

```python
import functools

import jax
import jax.numpy as jnp
from jax import lax
from jax.experimental import pallas as pl
from jax.experimental.pallas import tpu as pltpu

F32 = jnp.float32
BF16 = jnp.bfloat16

EPS = 1e-6
NEG = -1e30
ROPE_THETA = 10000.0
N_MIXERS = 3
A_GROUPS = ((128, 1), (512, 4), (2048, 16))
HEAD_DIM = 64
C_NOPE, C_ROPE, C_V = 64, 32, 64

LANES = 128
VMEM_LIMIT_BYTES = 56 * 1024 * 1024


def _params(*semantics):
    return pltpu.CompilerParams(dimension_semantics=semantics,
                                vmem_limit_bytes=VMEM_LIMIT_BYTES)


def _tile(m, candidates):
    for c in candidates:
        if m % c == 0:
            return c
    return m


def _rms(x, g):
    ms = jnp.mean(x * x, axis=-1, keepdims=True)
    return x * lax.rsqrt(ms + EPS) * g


def _dot(a, b):
    return jnp.dot(a, b, preferred_element_type=F32)


def _dot_nt(a, b):
    return lax.dot_general(a, b, (((1,), (1,)), ((), ())), preferred_element_type=F32)


def _swap_halves(y, half):
    width = y.shape[-1]
    lane = lax.broadcasted_iota(jnp.int32, y.shape, y.ndim - 1)
    first = (lane % (2 * half)) < half
    return jnp.where(first, pltpu.roll(y, width - half, y.ndim - 1), pltpu.roll(y, half, y.ndim - 1))


def _rope_tables(pos, dim, reps, pad_to=None):
    half = dim // 2
    inv = ROPE_THETA ** (-jnp.arange(half, dtype=F32) * 2.0 / dim)
    ang = pos.astype(F32)[:, None] * inv[None, :]
    cos, sin = jnp.cos(ang), jnp.sin(ang)
    c = jnp.concatenate([cos, cos], axis=-1)
    s = jnp.concatenate([-sin, sin], axis=-1)
    if pad_to is not None:
        z = jnp.zeros((pos.shape[0], pad_to - dim), F32)
        c = jnp.concatenate([c, z], axis=-1)
        s = jnp.concatenate([s, z], axis=-1)
    return jnp.tile(c, (1, reps)), jnp.tile(s, (1, reps))


def _ffn_kernel(x_ref, g_ref, wg_ref, wu_ref, wo_ref, o_ref, h_ref, acc_ref):
    j = pl.program_id(1)

    @pl.when(j == 0)
    def _():
        h_ref[...] = _rms(x_ref[...], g_ref[...]).astype(BF16)
        acc_ref[...] = jnp.zeros_like(acc_ref)

    h = h_ref[...]
    gate = _dot(h, wg_ref[...])
    up = _dot(h, wu_ref[...])
    act = (gate * jax.nn.sigmoid(gate) * up).astype(BF16)
    acc_ref[...] += _dot(act, wo_ref[...])

    @pl.when(j == pl.num_programs(1) - 1)
    def _():
        o_ref[...] = x_ref[...] + 0.5 * acc_ref[...]


def _ffn(x, g, w_in, w_out):
    m, d = x.shape
    f = w_out.shape[0]
    tm = _tile(m, (1024, 512, 256, 128, 64, 32, 16, 8))
    tf = _tile(f, (256, 128))
    nf = f // tf
    return pl.pallas_call(
        _ffn_kernel,
        grid=(m // tm, nf),
        in_specs=[
            pl.BlockSpec((tm, d), lambda i, j: (i, 0)),
            pl.BlockSpec((1, d), lambda i, j: (0, 0)),
            pl.BlockSpec((d, tf), lambda i, j: (0, j)),
            pl.BlockSpec((d, tf), lambda i, j: (0, j + nf)),
            pl.BlockSpec((tf, d), lambda i, j: (j, 0)),
        ],
        out_specs=pl.BlockSpec((tm, d), lambda i, j: (i, 0)),
        out_shape=jax.ShapeDtypeStruct((m, d), F32),
        scratch_shapes=[pltpu.VMEM((tm, d), BF16), pltpu.VMEM((tm, d), F32)],
        compiler_params=_params("parallel", "arbitrary"),
        name="ffn_half",
    )(x, g.reshape(1, d), w_in, w_in, w_out)


def _ple_kernel(x_ref, p_ref, g_ref, wg_ref, wp_ref, o_ref):
    x = x_ref[...]
    gate = jax.nn.sigmoid(_dot(_rms(x, g_ref[...]).astype(BF16), wg_ref[...]))
    o_ref[...] = x + gate * _dot(p_ref[...].astype(BF16), wp_ref[...])


def _ple_final_kernel(x_ref, p_ref, g_ref, wg_ref, wp_ref, fg_ref, o_ref):
    x = x_ref[...]
    gate = jax.nn.sigmoid(_dot(_rms(x, g_ref[...]).astype(BF16), wg_ref[...]))
    y = x + gate * _dot(p_ref[...].astype(BF16), wp_ref[...])
    o_ref[...] = _rms(y, fg_ref[...])


def _ple(x, p, g, w_gate, w_proj, final_g=None):
    m, d = x.shape
    pd = p.shape[1]
    tm = _tile(m, (512, 256, 128, 64, 32, 16, 8))
    row = lambda i: (i, 0)
    fix = lambda i: (0, 0)
    in_specs = [
        pl.BlockSpec((tm, d), row),
        pl.BlockSpec((tm, pd), row),
        pl.BlockSpec((1, d), fix),
        pl.BlockSpec((d, d), fix),
        pl.BlockSpec((pd, d), fix),
    ]
    args = [x, p, g.reshape(1, d), w_gate, w_proj]
    body = _ple_kernel
    if final_g is not None:
        in_specs.append(pl.BlockSpec((1, d), fix))
        args.append(final_g.reshape(1, d))
        body = _ple_final_kernel
    return pl.pallas_call(
        body,
        grid=(m // tm,),
        in_specs=in_specs,
        out_specs=pl.BlockSpec((tm, d), row),
        out_shape=jax.ShapeDtypeStruct((m, d), F32),
        compiler_params=_params("parallel"),
        name="ple",
    )(*args)


def _outproj_kernel(x_ref, o_ref_in, w_ref, y_ref):
    y_ref[...] = x_ref[...] + _dot(o_ref_in[...].astype(BF16), w_ref[...])


def _outproj(x, o, w):
    m, d = x.shape
    k = o.shape[1]
    tm = _tile(m, (512, 256, 128, 64, 32, 16, 8))
    return pl.pallas_call(
        _outproj_kernel,
        grid=(m // tm,),
        in_specs=[
            pl.BlockSpec((tm, d), lambda i: (i, 0)),
            pl.BlockSpec((tm, k), lambda i: (i, 0)),
            pl.BlockSpec((k, d), lambda i: (0, 0)),
        ],
        out_specs=pl.BlockSpec((tm, d), lambda i: (i, 0)),
        out_shape=jax.ShapeDtypeStruct((m, d), F32),
        compiler_params=_params("parallel"),
        name="outproj",
    )(x, o, w)


def _band_valid(tq, nk, offset, n_back):
    row = lax.broadcasted_iota(jnp.int32, (tq, nk), 0)
    col = lax.broadcasted_iota(jnp.int32, (tq, nk), 1)
    diff = row + offset - col
    return lax.bitcast_convert_type(diff, jnp.uint32) <= jnp.uint32(n_back)


def _a_combine_kernel(n_groups, *refs):
    x_ref = refs[0]
    o_refs = refs[1:1 + n_groups]
    l_refs = refs[1 + n_groups:1 + 2 * n_groups]
    w_ref, y_ref = refs[1 + 2 * n_groups:]
    lses = [r[...] for r in l_refs]
    mx = functools.reduce(jnp.maximum, lses)
    es = [jnp.exp(l - mx) for l in lses]
    den = functools.reduce(lambda a, b: a + b, es)
    num = functools.reduce(lambda a, b: a + b, [e * r[...] for e, r in zip(es, o_refs)])
    y_ref[...] = x_ref[...] + _dot((num / den).astype(BF16), w_ref[...])


def _a_combine(x, outs, lses, w_out):
    m, d = x.shape
    inner = w_out.shape[0]
    n_groups = len(outs)
    tm = _tile(m, (256, 128, 64, 32, 16, 8))
    row = lambda i: (i, 0)
    return pl.pallas_call(
        functools.partial(_a_combine_kernel, n_groups),
        grid=(m // tm,),
        in_specs=[pl.BlockSpec((tm, d), row)] + [pl.BlockSpec((tm, inner), row)] * (2 * n_groups)
        + [pl.BlockSpec((inner, d), lambda i: (0, 0))],
        out_specs=pl.BlockSpec((tm, d), row),
        out_shape=jax.ShapeDtypeStruct((m, d), F32),
        compiler_params=_params("parallel"),
        name="a_combine",
    )(x, *outs, *lses, w_out)


A_CHUNK = 512


def _rope_rows(x, cos, sin, n_heads):
    half = HEAD_DIM // 2
    parts = []
    for h in range(n_heads):
        parts.append(x[h * HEAD_DIM + half:(h + 1) * HEAD_DIM])
        parts.append(x[h * HEAD_DIM:h * HEAD_DIM + half])
    swapped = jnp.concatenate(parts, axis=0)
    return x * jnp.tile(cos, (n_heads, 1)) + swapped * jnp.tile(sin, (n_heads, 1))


def _rope_lanes(y, cos, sin):
    reps = y.shape[1] // LANES
    return y * jnp.tile(cos, (1, reps)) + _swap_halves(y, HEAD_DIM // 2) * jnp.tile(sin, (1, reps))


def _a_state_kernel(n_heads, x_ref, g_ref, wk_ref, wv_ref, cos_ref, sin_ref, *refs):
    o_ref = refs[-1]
    h = _rms(x_ref[...], g_ref[...]).astype(BF16)
    o_ref[0, 0, 0] = _rope_rows(_dot_nt(wk_ref[...], h), cos_ref[...], sin_ref[...], n_heads)
    o_ref[0, 0, 1] = _dot_nt(wv_ref[...], h)


def _a_state(x, g, wk_t, wv_t, cos_t, sin_t, prev, layer, n_layers, batch, n, window, n_heads):
    m, d = x.shape
    inner = n_heads * HEAD_DIM
    keep = min(window, n)
    tt = _tile(keep, (512, 256, 128))
    assert (n - keep) % tt == 0
    off = (n - keep) // tt
    fix = lambda b, t: (0, 0)
    in_specs = [
        pl.BlockSpec((tt, d), lambda b, t: (b * (n // tt) + off + t, 0)),
        pl.BlockSpec((1, d), fix),
        pl.BlockSpec((inner, d), fix),
        pl.BlockSpec((inner, d), fix),
        pl.BlockSpec((HEAD_DIM, tt), lambda b, t: (0, off + t)),
        pl.BlockSpec((HEAD_DIM, tt), lambda b, t: (0, off + t)),
    ]
    args = [x, g.reshape(1, d), wk_t, wv_t, cos_t, sin_t]
    aliases = {}
    if prev is not None:
        in_specs.append(pl.BlockSpec(memory_space=pl.ANY))
        args.append(prev)
        aliases = {len(args) - 1: 0}
    return pl.pallas_call(
        functools.partial(_a_state_kernel, n_heads),
        grid=(batch, keep // tt),
        in_specs=in_specs,
        out_specs=pl.BlockSpec((1, 1, 2, inner, tt), lambda b, t: (layer, b, 0, 0, t)),
        out_shape=jax.ShapeDtypeStruct((n_layers, batch, 2, inner, keep), F32),
        input_output_aliases=aliases,
        compiler_params=_params("parallel", "arbitrary"),
        name="a_state",
    )(*args)


def _a_prompt_kernel(scale, n_heads, n_back, halo, n_cls, x_ref, *refs):
    if halo:
        (xh_ref, g_ref, wq_ref, wk_ref, wv_ref, cq_ref, sq_ref, ck_ref, sk_ref, ckh_ref, skh_ref,
         o_ref, lse_ref, q_scr, kt_scr, vt_scr) = refs
    else:
        (g_ref, wq_ref, wk_ref, wv_ref, cq_ref, sq_ref, ck_ref, sk_ref,
         o_ref, lse_ref, q_scr, kt_scr, vt_scr) = refs
    c = pl.program_id(2)
    chunk = x_ref.shape[1]
    inner = n_heads * HEAD_DIM
    d = g_ref.shape[1]
    x = jnp.concatenate([x_ref[0, :, k * d:(k + 1) * d] for k in range(n_cls)], axis=0)
    cq = jnp.concatenate([cq_ref[:, k * LANES:(k + 1) * LANES] for k in range(n_cls)], axis=0)
    sq = jnp.concatenate([sq_ref[:, k * LANES:(k + 1) * LANES] for k in range(n_cls)], axis=0)
    ck = jnp.concatenate([ck_ref[k] for k in range(n_cls)], axis=1)
    sk = jnp.concatenate([sk_ref[k] for k in range(n_cls)], axis=1)
    h = _rms(x, g_ref[...]).astype(BF16)
    q_scr[...] = (_rope_lanes(_dot(h, wq_ref[...]), cq, sq) * scale).astype(BF16)
    kt_scr[:, halo:] = _rope_rows(_dot_nt(wk_ref[...], h), ck, sk, n_heads).astype(BF16)
    vt_scr[:, halo:] = _dot_nt(wv_ref[...], h).astype(BF16)
    if halo:
        hh = _rms(xh_ref[0], g_ref[...]).astype(BF16)
        kt_scr[:, :halo] = _rope_rows(_dot_nt(wk_ref[...], hh), ckh_ref[0], skh_ref[0], n_heads).astype(BF16)
        vt_scr[:, :halo] = _dot_nt(wv_ref[...], hh).astype(BF16)
    tq = min(LANES, chunk)
    low = lax.broadcasted_iota(jnp.int32, (tq, LANES), 1) < HEAD_DIM
    first_real = jnp.where(c > 0, 0, halo)

    def pair(hp, carry):
        r0 = pl.multiple_of(hp * LANES, LANES)
        for k in range(n_cls):
            for j in range(chunk // tq):
                q0 = j * tq
                if halo:
                    k0, nk, offset = q0, tq + halo, halo
                else:
                    k0 = max(0, q0 - n_back)
                    nk, offset = q0 + tq - k0, q0 - k0
                valid = _band_valid(tq, nk, offset, n_back)
                base = k * chunk
                qb = q_scr[base + q0:base + q0 + tq, pl.ds(r0, LANES)]
                kseg = kt_scr[pl.ds(r0, LANES), base + k0:base + k0 + nk]
                vseg = vt_scr[pl.ds(r0, LANES), base + k0:base + k0 + nk]
                outs, lses = [], []
                for head_low in (True, False):
                    sel = low if head_low else jnp.logical_not(low)
                    s = jnp.where(valid, _dot(jnp.where(sel, qb, jnp.zeros_like(qb)), kseg), NEG)
                    if halo and j == 0:
                        s = jnp.where(lax.broadcasted_iota(jnp.int32, (tq, nk), 1) >= first_real, s, NEG)
                    mx = jnp.max(s, axis=-1, keepdims=True)
                    p = jnp.exp(s - mx)
                    den = jnp.sum(p, axis=-1, keepdims=True)
                    outs.append(_dot_nt(p.astype(BF16), vseg) / den)
                    lses.append(mx + jnp.log(den))
                o_ref[0, q0:q0 + tq, pl.ds(k * inner + r0, LANES)] = jnp.where(low, outs[0], outs[1])
                lse_ref[0, q0:q0 + tq, pl.ds(k * inner + r0, LANES)] = jnp.where(low, lses[0], lses[1])
        return carry

    lax.fori_loop(0, inner // LANES, pair, 0)


def _a_prompt_group(x, g, wq, wk_t, wv_t, cos_l, sin_l, cos_r, sin_r, batch, n, gi, n_heads):
    window, dil = A_GROUPS[gi]
    n_back = window // dil
    d = x.shape[1]
    inner = n_heads * HEAD_DIM
    length = n // dil
    chunk = min(A_CHUNK, length)
    n_chunks = length // chunk
    halo = n_back if n_chunks > 1 else 0
    n_cls = max(1, min(dil, A_CHUNK // chunk))
    assert length % chunk == 0 and (halo == 0 or (chunk % halo == 0 and n_cls == 1)) and dil % n_cls == 0
    xv = x.reshape(batch, length, dil * d)
    clv = cos_l.reshape(length, dil * LANES)
    slv = sin_l.reshape(length, dil * LANES)
    fix = lambda b, r, c: (0, 0)
    prev = lambda c: jnp.maximum(c * (chunk // max(halo, 1)) - 1, 0)
    in_specs = [pl.BlockSpec((1, chunk, n_cls * d), lambda b, r, c: (b, c, r))]
    args = [xv]
    if halo:
        in_specs.append(pl.BlockSpec((1, halo, d), lambda b, r, c: (b, prev(c), r)))
        args.append(xv)
    in_specs += [
        pl.BlockSpec((1, d), fix),
        pl.BlockSpec((d, inner), fix),
        pl.BlockSpec((inner, d), fix),
        pl.BlockSpec((inner, d), fix),
        pl.BlockSpec((chunk, n_cls * LANES), lambda b, r, c: (c, r)),
        pl.BlockSpec((chunk, n_cls * LANES), lambda b, r, c: (c, r)),
        pl.BlockSpec((n_cls, HEAD_DIM, chunk), lambda b, r, c: (r, 0, c)),
        pl.BlockSpec((n_cls, HEAD_DIM, chunk), lambda b, r, c: (r, 0, c)),
    ]
    args += [g.reshape(1, d), wq, wk_t, wv_t, clv, slv, cos_r, sin_r]
    if halo:
        in_specs += [pl.BlockSpec((1, HEAD_DIM, halo), lambda b, r, c: (r, 0, prev(c)))] * 2
        args += [cos_r, sin_r]
    out_spec = pl.BlockSpec((1, chunk, n_cls * inner), lambda b, r, c: (b, c, r))
    shape = jax.ShapeDtypeStruct((batch, length, dil * inner), F32)
    tokens = n_cls * chunk
    o, lse = pl.pallas_call(
        functools.partial(_a_prompt_kernel, HEAD_DIM ** -0.5, n_heads, n_back, halo, n_cls),
        grid=(batch, dil // n_cls, n_chunks),
        in_specs=in_specs,
        out_specs=[out_spec, out_spec],
        out_shape=[shape, shape],
        scratch_shapes=[pltpu.VMEM((tokens, inner), BF16), pltpu.VMEM((inner, tokens + halo), BF16),
                        pltpu.VMEM((inner, tokens + halo), BF16)],
        compiler_params=_params("parallel", "parallel", "arbitrary"),
        name=f"a_prompt_g{gi}",
    )(*args)
    return o.reshape(batch * n, inner), lse.reshape(batch * n, inner)


def _a_sample_project_kernel(scale, n_heads, x_ref, g_ref, wq_ref, wk_ref, wv_ref, cq_ref, sq_ref, ck_ref, sk_ref,
                             q_ref, kv_ref):
    h = _rms(x_ref[...], g_ref[...]).astype(BF16)
    q_ref[...] = _rope_lanes(_dot(h, wq_ref[0]), cq_ref[...], sq_ref[...]) * scale
    kv_ref[0, 0] = _rope_rows(_dot_nt(wk_ref[0], h), ck_ref[...], sk_ref[...], n_heads)
    kv_ref[0, 1] = _dot_nt(wv_ref[0], h)


def _a_sample_project(x, g, wq, wk_t, wv_t, cos_l, sin_l, cos_r, sin_r, n_heads):
    t, d = x.shape
    n_groups = wq.shape[0]
    inner = n_heads * HEAD_DIM
    fix = lambda gi: (0, 0)
    return pl.pallas_call(
        functools.partial(_a_sample_project_kernel, HEAD_DIM ** -0.5, n_heads),
        grid=(n_groups,),
        in_specs=[
            pl.BlockSpec((t, d), fix),
            pl.BlockSpec((1, d), fix),
            pl.BlockSpec((1, d, inner), lambda gi: (gi, 0, 0)),
            pl.BlockSpec((1, inner, d), lambda gi: (gi, 0, 0)),
            pl.BlockSpec((1, inner, d), lambda gi: (gi, 0, 0)),
            pl.BlockSpec((t, LANES), fix),
            pl.BlockSpec((t, LANES), fix),
            pl.BlockSpec((HEAD_DIM, t), fix),
            pl.BlockSpec((HEAD_DIM, t), fix),
        ],
        out_specs=[pl.BlockSpec((t, inner), lambda gi: (0, gi)),
                   pl.BlockSpec((1, 2, inner, t), lambda gi: (gi, 0, 0, 0))],
        out_shape=[jax.ShapeDtypeStruct((t, n_groups * inner), F32),
                   jax.ShapeDtypeStruct((n_groups, 2, inner, t), F32)],
        compiler_params=_params("arbitrary"),
        name="a_sample_project",
    )(x, g.reshape(1, d), wq, wk_t, wv_t, cos_l, sin_l, cos_r, sin_r)


A_SAMPLE_BLOCK_BYTES = 4 * 1024 * 1024
SUBLANES = 8


def _a_sample_kernel(n_q, dil, n_back, hpb, q_ref, new_ref, buf_ref, *refs):
    o_ref, lse_ref, st_ref = refs[-3:]
    b = pl.program_id(0)
    width = hpb * HEAD_DIM
    slots = max(hpb, SUBLANES)
    buf_len = buf_ref.shape[-1]
    n_tok = new_ref.shape[-1]
    buf = buf_ref[0, 0].reshape(2 * width, buf_len)
    new = new_ref[0].reshape(2 * width, n_tok)
    new_sel = pltpu.roll(new, n_tok - n_q - b * n_q, 1)
    lane_w = lax.broadcasted_iota(jnp.int32, (2 * width, buf_len), 1)
    state = jnp.where(lane_w >= buf_len - n_q, jnp.tile(new_sel, (1, buf_len // n_tok)),
                      pltpu.roll(buf, buf_len - n_q, 1))
    st_ref[0, 0] = state.reshape(2, width, buf_len)

    rows = n_q * slots
    own = (lax.broadcasted_iota(jnp.int32, (slots, width), 1) // HEAD_DIM
           == lax.broadcasted_iota(jnp.int32, (slots, width), 0))
    q = q_ref[0]
    qbd = jnp.concatenate(
        [jnp.where(own, jnp.broadcast_to(q[i:i + 1, :], (slots, width)), 0.0) for i in range(n_q)],
        axis=0).astype(BF16)

    def masked(s, diff):
        s = jnp.where((diff & (dil - 1)) == 0, s, NEG)
        return jnp.where(lax.bitcast_convert_type(diff, jnp.uint32) <= jnp.uint32(n_back * dil), s, NEG)

    qi = lax.broadcasted_iota(jnp.int32, (rows, buf_len), 0) // slots
    s_buf = masked(_dot(qbd, buf[:width].astype(BF16)),
                   buf_len + qi - lax.broadcasted_iota(jnp.int32, (rows, buf_len), 1))
    qi = lax.broadcasted_iota(jnp.int32, (rows, n_tok), 0) // slots
    lane = lax.broadcasted_iota(jnp.int32, (rows, n_tok), 1)
    s_new = masked(_dot(qbd, new_sel[:width].astype(BF16)),
                   jnp.where(lane >= n_tok - n_q, qi - (lane - (n_tok - n_q)), -1))
    mx = jnp.maximum(jnp.max(s_buf, axis=-1, keepdims=True), jnp.max(s_new, axis=-1, keepdims=True))
    p_buf = jnp.exp(s_buf - mx)
    p_new = jnp.exp(s_new - mx)
    den = jnp.sum(p_buf, axis=-1, keepdims=True) + jnp.sum(p_new, axis=-1, keepdims=True)
    out = (_dot_nt(p_buf.astype(BF16), buf[width:].astype(BF16))
           + _dot_nt(p_new.astype(BF16), new_sel[width:].astype(BF16))) / den
    lse = jnp.broadcast_to(mx + jnp.log(den), (rows, width))
    for i in range(n_q):
        sl = slice(i * slots, (i + 1) * slots)
        o_ref[0, i:i + 1, :] = jnp.sum(jnp.where(own, out[sl], 0.0), axis=0, keepdims=True)
        lse_ref[0, i:i + 1, :] = jnp.sum(jnp.where(own, lse[sl], 0.0), axis=0, keepdims=True)


def _a_sample_group(q, kv_new, buf_t, prev, layer, n_layers, dec_batch, n_q, gi, n_groups, n_heads):
    window, dil = A_GROUPS[gi]
    n_back = window // dil
    inner = n_heads * HEAD_DIM
    buf_len = buf_t.shape[-1]
    n_tok = kv_new.shape[-1]
    hpb = n_heads
    while hpb > 1 and 2 * hpb * HEAD_DIM * buf_len * 4 > A_SAMPLE_BLOCK_BYTES:
        hpb //= 2
    width = hpb * HEAD_DIM
    nhb = inner // width
    assert buf_len == window and buf_len % n_tok == 0 and n_tok == dec_batch * n_q and dil & (dil - 1) == 0
    in_specs = [
        pl.BlockSpec((1, n_q, width), lambda b, hb: (b, 0, gi * nhb + hb)),
        pl.BlockSpec((1, 2, width, n_tok), lambda b, hb: (gi, 0, hb, 0)),
        pl.BlockSpec((1, 1, 2, width, buf_len), lambda b, hb: (layer, b, 0, hb, 0)),
    ]
    args = [q, kv_new, buf_t]
    aliases = {}
    if prev is not None:
        in_specs.append(pl.BlockSpec(memory_space=pl.ANY))
        args.append(prev)
        aliases = {len(args) - 1: 2}
    out_spec = pl.BlockSpec((1, n_q, width), lambda b, hb: (b, 0, hb))
    shape = jax.ShapeDtypeStruct((dec_batch, n_q, inner), F32)
    o, lse, st = pl.pallas_call(
        functools.partial(_a_sample_kernel, n_q, dil, n_back, hpb),
        grid=(dec_batch, nhb),
        in_specs=in_specs,
        out_specs=[out_spec, out_spec,
                   pl.BlockSpec((1, 1, 2, width, buf_len), lambda b, hb: (layer, b, 0, hb, 0))],
        out_shape=[shape, shape, jax.ShapeDtypeStruct((n_layers, dec_batch, 2, inner, buf_len), F32)],
        input_output_aliases=aliases,
        compiler_params=_params("parallel", "arbitrary"),
        name=f"a_sample_g{gi}",
    )(*args)
    return o.reshape(dec_batch * n_q, inner), lse.reshape(dec_batch * n_q, inner), st


def _mixer_a(xp, xs, g, w_in, w_out, bufs_t, prev_p, prev_s, layer, n_layers, batch, n, dec_batch, n_q, past_len):
    n_groups = len(A_GROUPS)
    d = xp.shape[1]
    inner = w_out.shape[0]
    n_heads = inner // HEAD_DIM
    w4 = w_in.reshape(d, n_groups, 3, inner)
    wq = jnp.transpose(w4[:, :, 0], (1, 0, 2)).astype(BF16)
    wk_t = jnp.transpose(w4[:, :, 1], (1, 2, 0)).astype(BF16)
    wv_t = jnp.transpose(w4[:, :, 2], (1, 2, 0)).astype(BF16)
    w_o = w_out.astype(BF16)

    cos_p, sin_p = _rope_tables(jnp.arange(n), HEAD_DIM, 1)
    cos_pl, sin_pl = jnp.tile(cos_p, (1, 2)), jnp.tile(sin_p, (1, 2))
    outs_p, lses_p, st_p = [], [], []
    for gi, (window, dil) in enumerate(A_GROUPS):
        cos_r = jnp.transpose(cos_p.reshape(n // dil, dil, HEAD_DIM), (1, 2, 0))
        sin_r = jnp.transpose(sin_p.reshape(n // dil, dil, HEAD_DIM), (1, 2, 0))
        o, lse = _a_prompt_group(xp, g, wq[gi], wk_t[gi], wv_t[gi], cos_pl, sin_pl, cos_r, sin_r,
                                 batch, n, gi, n_heads)
        outs_p.append(o)
        lses_p.append(lse)
        st_p.append(_a_state(xp, g, wk_t[gi], wv_t[gi], cos_p.T, sin_p.T,
                             None if prev_p is None else prev_p[gi], layer, n_layers, batch, n, window, n_heads))

    cos_s, sin_s = _rope_tables(past_len + jnp.arange(n_q), HEAD_DIM, 1)
    cos_s, sin_s = jnp.tile(cos_s, (dec_batch, 1)), jnp.tile(sin_s, (dec_batch, 1))
    q_s, kv_new = _a_sample_project(xs, g, wq, wk_t, wv_t, jnp.tile(cos_s, (1, 2)), jnp.tile(sin_s, (1, 2)),
                                    cos_s.T, sin_s.T, n_heads)
    q_s = q_s.reshape(dec_batch, n_q, n_groups * inner)
    outs_s, lses_s, st_s = [], [], []
    for gi in range(n_groups):
        o, lse, st = _a_sample_group(q_s, kv_new, bufs_t[gi], None if prev_s is None else prev_s[gi],
                                     layer, n_layers, dec_batch, n_q, gi, n_groups, n_heads)
        outs_s.append(o)
        lses_s.append(lse)
        st_s.append(st)
    xp = _a_combine(xp, outs_p, lses_p, w_o)
    xs = _a_combine(xs, outs_s, lses_s, w_o)
    return xp, xs, st_p, st_s


def _rms_matmul_kernel(x_ref, g_ref, w_ref, o_ref, h_ref):
    @pl.when(pl.program_id(1) == 0)
    def _():
        h_ref[...] = _rms(x_ref[...], g_ref[...]).astype(BF16)

    o_ref[...] = _dot(h_ref[...], w_ref[...])


def _rms_matmul(x, g, w):
    m, d = x.shape
    n_out = w.shape[1]
    tm = _tile(m, (512, 256, 128, 64, 32, 16, 8))
    tn = _tile(n_out, (640, 512, 256, 128))
    return pl.pallas_call(
        _rms_matmul_kernel,
        grid=(m // tm, n_out // tn),
        in_specs=[
            pl.BlockSpec((tm, d), lambda i, j: (i, 0)),
            pl.BlockSpec((1, d), lambda i, j: (0, 0)),
            pl.BlockSpec((d, tn), lambda i, j: (0, j)),
        ],
        out_specs=pl.BlockSpec((tm, tn), lambda i, j: (i, j)),
        out_shape=jax.ShapeDtypeStruct((m, n_out), F32),
        scratch_shapes=[pltpu.VMEM((tm, d), BF16)],
        compiler_params=_params("parallel", "arbitrary"),
        name="rms_matmul",
    )(x, g.reshape(1, d), w)


def _flash_kernel(n_stack, dk, dv, v_from_k, tk, *refs):
    if v_from_k:
        q_ref, k_ref, o_ref, m_scr, l_scr, acc_scr = refs
        v_ref = k_ref
    else:
        q_ref, k_ref, v_ref, o_ref, m_scr, l_scr, acc_scr = refs
    tq = q_ref.shape[1]
    rows = n_stack * tq
    i = pl.program_id(2)
    qb = q_ref[0]
    q = qb if n_stack == 1 else jnp.concatenate([qb[:, s * dk:(s + 1) * dk] for s in range(n_stack)], axis=0)
    m_scr[...] = jnp.full(m_scr.shape, NEG, F32)
    l_scr[...] = jnp.zeros(l_scr.shape, F32)
    acc_scr[...] = jnp.zeros(acc_scr.shape, F32)

    def step(kb, masked):
        k0 = pl.multiple_of(kb * tk, tk)
        kblk = k_ref[0, pl.ds(k0, tk), :]
        vblk = kblk[:, :dv] if v_from_k else v_ref[0, pl.ds(k0, tk), :]
        s = _dot_nt(q, kblk)
        if masked:
            qpos = i * tq + lax.broadcasted_iota(jnp.int32, (rows, tk), 0) % tq
            col = lax.broadcasted_iota(jnp.int32, (rows, tk), 1)
            s = jnp.where(col + k0 <= qpos, s, NEG)
        m_old = m_scr[...]
        m_new = jnp.maximum(m_old, jnp.max(s, axis=-1, keepdims=True))
        alpha = jnp.exp(m_old - m_new)
        p = jnp.exp(s - jnp.tile(m_new, (1, tk // LANES)))
        l_scr[...] = alpha * l_scr[...] + jnp.sum(p, axis=-1, keepdims=True)
        acc_scr[...] = jnp.tile(alpha, (1, dv // LANES)) * acc_scr[...] + _dot(p.astype(BF16), vblk)
        m_scr[...] = m_new

    def full_body(kb, carry):
        step(kb, False)
        return carry

    def diag_body(kb, carry):
        step(kb, True)
        return carry

    n_full = lax.div(i * tq + 1, tk)
    n_total = lax.div((i + 1) * tq + tk - 1, tk)
    lax.fori_loop(0, n_full, full_body, 0)
    lax.fori_loop(n_full, n_total, diag_body, 0)
    out = acc_scr[...] / jnp.tile(l_scr[...], (1, dv // LANES))
    for s in range(n_stack):
        o_ref[0, :, s * dv:(s + 1) * dv] = out[s * tq:(s + 1) * tq, :]


def _flash(q, k, v, n_kv_heads, n_stack, dk, dv, tq, tk):
    batch, n, _ = q.shape
    tq = min(tq, n)
    tk = min(tk, n)
    assert n % tq == 0 and n % tk == 0 and tk % LANES == 0 and dv % LANES == 0
    v_from_k = v is None
    in_specs = [
        pl.BlockSpec((1, tq, n_stack * dk), lambda b, h, i: (b, i, h)),
        pl.BlockSpec((1, n, dk), lambda b, h, i: (b, 0, h)),
    ]
    args = [q, k]
    if not v_from_k:
        in_specs.append(pl.BlockSpec((1, n, dv), lambda b, h, i: (b, 0, h)))
        args.append(v)
    rows = n_stack * tq
    return pl.pallas_call(
        functools.partial(_flash_kernel, n_stack, dk, dv, v_from_k, tk),
        grid=(batch, n_kv_heads, n // tq),
        in_specs=in_specs,
        out_specs=pl.BlockSpec((1, tq, n_stack * dv), lambda b, h, i: (b, i, h)),
        out_shape=jax.ShapeDtypeStruct((batch, n, n_kv_heads * n_stack * dv), F32),
        scratch_shapes=[pltpu.VMEM((rows, LANES), F32), pltpu.VMEM((rows, LANES), F32),
                        pltpu.VMEM((rows, dv), F32)],
        compiler_params=_params("parallel", "parallel", "arbitrary"),
        name="flash",
    )(*args)


B_PAGES_PER_STEP = 8
C_PAGES_PER_STEP = 16


def _log_sigmoid(z):
    return jnp.minimum(z, 0.0) - jnp.log1p(jnp.exp(-jnp.abs(z)))


def _split3(c):
    hi = c.astype(BF16)
    r = c - hi.astype(F32)
    mid = r.astype(BF16)
    lo = (r - mid.astype(F32)).astype(BF16)
    return hi, mid, lo


def _pad_heads(y):
    t, w = y.shape
    low = lax.broadcasted_iota(jnp.int32, (t, LANES), 1) < HEAD_DIM
    pieces = []
    for p in range(w // LANES):
        blk = y[:, p * LANES:(p + 1) * LANES]
        pieces.append(jnp.where(low, blk, 0.0))
        pieces.append(jnp.where(low, pltpu.roll(blk, HEAD_DIM, 1), 0.0))
    return jnp.concatenate(pieces, axis=1)


def _b_project_kernel(scale, n_heads, x_ref, g_ref, w_ref, wkv_t_ref, wf_ref, bias_ref, pq_ref, pk_ref, oq_ref, ok_ref,
                      kvt_ref, lf_ref, qa_ref, ka_ref, va_ref, carry_ref):
    @pl.when(pl.program_id(1) == 0)
    def _():
        carry_ref[...] = jnp.zeros_like(carry_ref)

    inner = n_heads * HEAD_DIM
    h = _rms(x_ref[...], g_ref[...]).astype(BF16)
    y = _dot(h, w_ref[...])
    z = _dot(h, wf_ref[...]) + bias_ref[...]
    tm = z.shape[0]
    lane = lax.broadcasted_iota(jnp.int32, (tm, LANES), 1)
    lf = jnp.where(lane < n_heads, _log_sigmoid(z), 0.0)
    lf_ref[...] = lf[:, :n_heads]
    tri = (lax.broadcasted_iota(jnp.int32, (tm, tm), 0) >= lax.broadcasted_iota(jnp.int32, (tm, tm), 1))
    tri = jnp.where(tri, 1.0, 0.0).astype(BF16)
    hi, mid, lo = _split3(lf)
    c = _dot(tri, hi) + _dot(tri, mid) + _dot(tri, lo) + carry_ref[...]
    carry_ref[...] = c[tm - 1:tm, :]
    c3 = jnp.concatenate(_split3(c), axis=1)
    q_aug = _dot(c3, pq_ref[...]) + oq_ref[...]
    k_aug = _dot(c3, pk_ref[...]) + ok_ref[...]
    q = y[:, :inner] * scale
    k = y[:, inner:2 * inner]
    v = y[:, 2 * inner:]
    kvt_ref[0, 0] = _dot_nt(wkv_t_ref[0], h)
    kvt_ref[0, 1] = _dot_nt(wkv_t_ref[1], h)
    qa_ref[...] = (_pad_heads(q) + q_aug).astype(BF16)
    ka_ref[...] = (_pad_heads(k) + k_aug).astype(BF16)
    va_ref[...] = _pad_heads(v).astype(BF16)


def _b_aug_constants(n_heads):
    pq = jnp.zeros((3 * LANES, n_heads * LANES), F32)
    pk = jnp.zeros((3 * LANES, n_heads * LANES), F32)
    oq = jnp.zeros((1, n_heads * LANES), F32)
    ok = jnp.zeros((1, n_heads * LANES), F32)
    heads = jnp.arange(n_heads)
    for piece in range(3):
        pq = pq.at[piece * LANES + heads, heads * LANES + HEAD_DIM + piece].set(1.0)
        pk = pk.at[piece * LANES + heads, heads * LANES + HEAD_DIM + 3 + piece].set(-1.0)
        ok = ok.at[0, heads * LANES + HEAD_DIM + piece].set(1.0)
        oq = oq.at[0, heads * LANES + HEAD_DIM + 3 + piece].set(1.0)
    return pq.astype(BF16), pk.astype(BF16), oq, ok


def _b_project(x, g, w_qkv, wkv_t, w_f, bias, batch, n, n_heads):
    m, d = x.shape
    inner = n_heads * HEAD_DIM
    tm = _tile(n, (256, 128, 64, 32, 16, 8))
    nt = n // tm
    pq, pk, oq, ok = _b_aug_constants(n_heads)
    row = lambda b, t: (b * nt + t, 0)
    fix = lambda b, t: (0, 0)
    wide = n_heads * LANES
    return pl.pallas_call(
        functools.partial(_b_project_kernel, HEAD_DIM ** -0.5, n_heads),
        grid=(batch, nt),
        in_specs=[
            pl.BlockSpec((tm, d), row),
            pl.BlockSpec((1, d), fix),
            pl.BlockSpec((d, 3 * inner), fix),
            pl.BlockSpec((2, inner, d), lambda b, t: (0, 0, 0)),
            pl.BlockSpec((d, LANES), fix),
            pl.BlockSpec((1, LANES), fix),
            pl.BlockSpec((3 * LANES, wide), fix),
            pl.BlockSpec((3 * LANES, wide), fix),
            pl.BlockSpec((1, wide), fix),
            pl.BlockSpec((1, wide), fix),
        ],
        out_specs=[
            pl.BlockSpec((1, 2, inner, tm), lambda b, t: (b, 0, 0, t)),
            pl.BlockSpec((tm, n_heads), row),
            pl.BlockSpec((tm, wide), row),
            pl.BlockSpec((tm, wide), row),
            pl.BlockSpec((tm, wide), row),
        ],
        out_shape=[
            jax.ShapeDtypeStruct((batch, 2, inner, n), F32),
            jax.ShapeDtypeStruct((m, n_heads), F32),
            jax.ShapeDtypeStruct((m, wide), BF16),
            jax.ShapeDtypeStruct((m, wide), BF16),
            jax.ShapeDtypeStruct((m, wide), BF16),
        ],
        scratch_shapes=[pltpu.VMEM((1, LANES), F32)],
        compiler_params=_params("parallel", "arbitrary"),
        name="b_project",
    )(x, g.reshape(1, d), w_qkv, wkv_t, w_f, bias, pq, pk, oq, ok)


def _online_update(s, m_scr, l_scr, acc_scr, pv):
    m_old = m_scr[...]
    m_new = jnp.maximum(m_old, jnp.max(s, axis=-1, keepdims=True))
    alpha = jnp.exp(m_old - m_new)
    p = jnp.exp(s - jnp.tile(m_new, (1, s.shape[1] // LANES)))
    l_scr[...] = alpha * l_scr[...] + jnp.sum(p, axis=-1, keepdims=True)
    acc_scr[...] = jnp.tile(alpha, (1, acc_scr.shape[1] // LANES)) * acc_scr[...] + pv(p.astype(BF16))
    m_scr[...] = m_new


def _init_softmax(s, m_scr, l_scr, acc_scr, pv):
    mx = jnp.max(s, axis=-1, keepdims=True)
    pr = jnp.exp(s - mx)
    m_scr[...] = jnp.broadcast_to(mx, m_scr.shape)
    l_scr[...] = jnp.broadcast_to(jnp.sum(pr, axis=-1, keepdims=True), l_scr.shape)
    acc_scr[...] = pv(pr.astype(BF16))


def _b_sample_kernel(scale, n_q, n_heads, n_group, pt_ref, y_ref, bias_ref, *refs):
    kt_refs = refs[:n_group]
    vt_refs = refs[n_group:2 * n_group]
    lft_refs = refs[2 * n_group:3 * n_group]
    (o_ref, lfo_ref, qbd_scr, at_scr, m_scr, l_scr, acc_scr, tail_scr, kn_scr, vn_scr) = refs[3 * n_group:]
    p = pl.program_id(1)
    inner = n_heads * HEAD_DIM
    rows = n_q * n_heads
    slots = kt_refs[0].shape[3]

    @pl.when(p == 0)
    def _():
        y = y_ref[0]
        q = y[:, :inner] * scale
        z = y[:, 3 * inner:] + bias_ref[...]
        lane = lax.broadcasted_iota(jnp.int32, (n_q, LANES), 1)
        lf = jnp.where(lane < n_heads, _log_sigmoid(z), 0.0)
        lfo_ref[0] = lf
        run = [lf[0:1]]
        for i in range(1, n_q):
            run.append(run[-1] + lf[i:i + 1])
        eye = (lax.broadcasted_iota(jnp.int32, (n_heads, LANES), 0)
               == lax.broadcasted_iota(jnp.int32, (n_heads, LANES), 1))
        cols = [jnp.sum(jnp.where(eye, jnp.broadcast_to(r, (n_heads, LANES)), 0.0), axis=1, keepdims=True)
                for r in run]
        a_col = jnp.concatenate(cols, axis=0)
        at_scr[...] = jnp.broadcast_to(a_col, at_scr.shape)
        lane_head = lax.broadcasted_iota(jnp.int32, (n_heads, inner), 1) // HEAD_DIM
        own = lane_head == lax.broadcasted_iota(jnp.int32, (n_heads, inner), 0)
        qbd = jnp.concatenate(
            [jnp.where(own, jnp.broadcast_to(q[i:i + 1, :], (n_heads, inner)), 0.0) for i in range(n_q)],
            axis=0).astype(BF16)
        qbd_scr[...] = qbd
        kn_scr[...] = jnp.zeros_like(kn_scr)
        vn_scr[...] = jnp.zeros_like(vn_scr)
        kn_scr[0:n_q, :] = y[:, inner:2 * inner]
        vn_scr[0:n_q, :] = y[:, 2 * inner:3 * inner]
        s = _dot_nt(qbd, kn_scr[...].astype(BF16))
        col = lax.broadcasted_iota(jnp.int32, (rows, LANES), 1)
        a_key = jnp.zeros((rows, LANES), F32)
        for i in range(n_q):
            a_key = jnp.where(col == i, jnp.concatenate([cols[i]] * n_q, axis=0), a_key)
        s = s + a_col - a_key
        qi = lax.broadcasted_iota(jnp.int32, (rows, LANES), 0) // n_heads
        s = jnp.where(col <= qi, s, NEG)
        vn = vn_scr[...].astype(BF16)
        _init_softmax(s, m_scr, l_scr, acc_scr, lambda pb: _dot(pb, vn))
        tail_scr[...] = jnp.zeros_like(tail_scr)

    lfts = [r[0] for r in lft_refs]
    later = (lax.broadcasted_iota(jnp.int32, (slots, slots), 0) > lax.broadcasted_iota(jnp.int32, (slots, slots), 1))
    later = jnp.where(later, 1.0, 0.0).astype(BF16)
    pieces = [piece for lft in lfts for piece in _split3(lft)]
    r3 = _dot(jnp.concatenate(pieces, axis=0), later)
    qbd = qbd_scr[...]
    tail = tail_scr[...]
    parts = []
    for gi in range(n_group):
        base = 3 * n_heads * gi
        r = (r3[base:base + n_heads] + r3[base + n_heads:base + 2 * n_heads]
             + r3[base + 2 * n_heads:base + 3 * n_heads] + tail)
        parts.append(_dot(qbd, kt_refs[gi][0, 0].astype(BF16)) + jnp.concatenate([r] * n_q, axis=0))
        tail = tail + jnp.sum(lfts[gi], axis=1, keepdims=True)
    tail_scr[...] = tail
    s = jnp.concatenate(parts, axis=1) + jnp.tile(at_scr[...], (1, n_group * slots // LANES))

    def pv(pb):
        out = _dot_nt(pb[:, :slots], vt_refs[0][0, 0].astype(BF16))
        for gi in range(1, n_group):
            out = out + _dot_nt(pb[:, gi * slots:(gi + 1) * slots], vt_refs[gi][0, 0].astype(BF16))
        return out

    _online_update(s, m_scr, l_scr, acc_scr, pv)

    @pl.when(p == pl.num_programs(1) - 1)
    def _():
        lane_head = lax.broadcasted_iota(jnp.int32, (n_heads, inner), 1) // HEAD_DIM
        own = lane_head == lax.broadcasted_iota(jnp.int32, (n_heads, inner), 0)
        out = acc_scr[...] / jnp.tile(l_scr[...], (1, inner // LANES))
        for i in range(n_q):
            o_ref[0, i:i + 1, :] = jnp.sum(jnp.where(own, out[i * n_heads:(i + 1) * n_heads], 0.0),
                                           axis=0, keepdims=True)


def _b_sample(y, bias, cache_kv, cache_logf, page_table, dec_batch, n_q, n_heads, n_group):
    inner = n_heads * HEAD_DIM
    pool, slots = cache_kv.shape[0], cache_kv.shape[1]
    n_pages = page_table.shape[1]
    n_group = min(n_group, n_pages)
    assert n_pages % n_group == 0 and slots % LANES == 0
    kvt = jnp.transpose(cache_kv, (0, 2, 3, 4, 1)).reshape(pool, 2, inner, slots)
    lft = jnp.transpose(cache_logf, (0, 2, 1))
    yv = y.reshape(dec_batch, n_q, 3 * inner + LANES)
    rows = n_q * n_heads

    def page(b, p, pt, gi):
        return pt[b, n_pages - 1 - (p * n_group + gi)]

    in_specs = [
        pl.BlockSpec((1, n_q, 3 * inner + LANES), lambda b, p, pt: (b, 0, 0)),
        pl.BlockSpec((1, LANES), lambda b, p, pt: (0, 0)),
    ]
    for kind in (0, 1):
        for gi in range(n_group):
            in_specs.append(pl.BlockSpec((1, 1, inner, slots),
                                         lambda b, p, pt, gi=gi, kind=kind: (page(b, p, pt, gi), kind, 0, 0)))
    for gi in range(n_group):
        in_specs.append(pl.BlockSpec((1, n_heads, slots), lambda b, p, pt, gi=gi: (page(b, p, pt, gi), 0, 0)))
    grid_spec = pltpu.PrefetchScalarGridSpec(
        num_scalar_prefetch=1,
        grid=(dec_batch, n_pages // n_group),
        in_specs=in_specs,
        out_specs=[
            pl.BlockSpec((1, n_q, inner), lambda b, p, pt: (b, 0, 0)),
            pl.BlockSpec((1, n_q, LANES), lambda b, p, pt: (b, 0, 0)),
        ],
        scratch_shapes=[
            pltpu.VMEM((rows, inner), BF16),
            pltpu.VMEM((rows, LANES), F32),
            pltpu.VMEM((rows, LANES), F32),
            pltpu.VMEM((rows, LANES), F32),
            pltpu.VMEM((rows, inner), F32),
            pltpu.VMEM((n_heads, 1), F32),
            pltpu.VMEM((LANES, inner), F32),
            pltpu.VMEM((LANES, inner), F32),
        ],
    )
    o, lf = pl.pallas_call(
        functools.partial(_b_sample_kernel, HEAD_DIM ** -0.5, n_q, n_heads, n_group),
        grid_spec=grid_spec,
        out_shape=[jax.ShapeDtypeStruct((dec_batch, n_q, inner), F32),
                   jax.ShapeDtypeStruct((dec_batch, n_q, LANES), F32)],
        compiler_params=_params("parallel", "arbitrary"),
        name="b_sample",
    )(page_table, yv, bias, *([kvt] * (2 * n_group)), *([lft] * n_group))
    return o.reshape(dec_batch * n_q, inner), lf[:, :, :n_heads]


def _mixer_b(xp, xs, g, w_in, f_bias, w_out, cache_kv, cache_logf, page_table, batch, n, dec_batch, n_q):
    inner = w_out.shape[0]
    n_heads = inner // HEAD_DIM
    d = xp.shape[1]
    w_qkv = w_in[:, :3 * inner].astype(BF16)
    w_f = jnp.pad(w_in[:, 3 * inner:], ((0, 0), (0, LANES - n_heads))).astype(BF16)
    bias = jnp.pad(f_bias.reshape(1, n_heads), ((0, 0), (0, LANES - n_heads)))
    w_out_pad = jnp.pad(w_out.reshape(n_heads, HEAD_DIM, d),
                        ((0, 0), (0, LANES - HEAD_DIM), (0, 0))).reshape(n_heads * LANES, d).astype(BF16)

    wkv_t = jnp.transpose(w_in[:, inner:3 * inner].reshape(d, 2, inner), (1, 2, 0)).astype(BF16)
    kvt_p, lf_p, qa, ka, va = _b_project(xp, g, w_qkv, wkv_t, w_f, bias, batch, n, n_heads)
    kv_p = jnp.transpose(kvt_p.reshape(batch, 2, n_heads, HEAD_DIM, n), (0, 4, 1, 2, 3))
    wide = n_heads * LANES
    o_p = _flash(qa.reshape(batch, n, wide), ka.reshape(batch, n, wide), va.reshape(batch, n, wide),
                 n_heads, 1, LANES, LANES, 512, 512)
    xp = _outproj(xp, o_p.reshape(batch * n, wide), w_out_pad)

    y_s = _rms_matmul(xs, g, jnp.concatenate([w_qkv, w_f], axis=1))
    o_s, lf_s = _b_sample(y_s, bias, cache_kv, cache_logf, page_table, dec_batch, n_q, n_heads, B_PAGES_PER_STEP)
    xs = _outproj(xs, o_s, w_out.astype(BF16))
    kv_s = y_s[:, inner:3 * inner].reshape(dec_batch, n_q, 2, n_heads, HEAD_DIM)
    return (xp, xs, kv_p, lf_p.reshape(batch, n, n_heads), kv_s, lf_s)


def _c_project_kernel(scale, n_heads, x_ref, g_ref, wq_ref, wkv_ref, wpe_ref, qn_ref, kvn_ref, wqn_ref, wqr_ref,
                      wuk_ref, cos_ref, sin_ref, st_ref, kc_ref, qa_ref):
    h = _rms(x_ref[...], g_ref[...]).astype(BF16)
    cq = _rms(_dot(h, wq_ref[...]), qn_ref[...]).astype(BF16)
    ckv = _rms(_dot(h, wkv_ref[...]), kvn_ref[...])
    cos, sin = cos_ref[...], sin_ref[...]
    ype = _dot(h, wpe_ref[...])
    kpe = ype * cos + _swap_halves(ype, C_ROPE // 2) * sin
    lat = ckv.shape[1]
    st_ref[:, :lat] = ckv
    st_ref[:, lat:lat + C_ROPE] = kpe[:, :C_ROPE]
    kc_ref[:, :lat] = ckv.astype(BF16)
    kc_ref[:, lat:] = kpe.astype(BF16)
    qn = _dot(cq, wqn_ref[...]).astype(BF16)
    qr = _dot(cq, wqr_ref[...])
    dq = lat + LANES
    for pair in range(n_heads // 2):
        ql = _dot(qn[:, pair * LANES:(pair + 1) * LANES], wuk_ref[pair])
        for j in range(2):
            head = 2 * pair + j
            r = qr[:, head * LANES:(head + 1) * LANES]
            r = r * cos + _swap_halves(r, C_ROPE // 2) * sin
            qa_ref[:, head * dq:head * dq + lat] = (ql[:, j * lat:(j + 1) * lat] * scale).astype(BF16)
            qa_ref[:, head * dq + lat:(head + 1) * dq] = (r * scale).astype(BF16)


def _c_project(x, g, w, q_norm, kv_norm, cos_t, sin_t, n_heads):
    m, d = x.shape
    lat = w["kv"].shape[1]
    q_lora = w["q"].shape[1]
    p = cos_t.shape[0]
    tm = _tile(p, (256, 128, 64, 32, 16, 8))
    npb = p // tm
    dq = lat + LANES
    row = lambda i: (i, 0)
    fix = lambda i: (0, 0)
    tab = lambda i: (i % npb, 0)
    return pl.pallas_call(
        functools.partial(_c_project_kernel, (C_NOPE + C_ROPE) ** -0.5, n_heads),
        grid=(m // tm,),
        in_specs=[
            pl.BlockSpec((tm, d), row),
            pl.BlockSpec((1, d), fix),
            pl.BlockSpec((d, q_lora), fix),
            pl.BlockSpec((d, lat), fix),
            pl.BlockSpec((d, LANES), fix),
            pl.BlockSpec((1, q_lora), fix),
            pl.BlockSpec((1, lat), fix),
            pl.BlockSpec((q_lora, n_heads * C_NOPE), fix),
            pl.BlockSpec((q_lora, n_heads * LANES), fix),
            pl.BlockSpec((n_heads // 2, LANES, 2 * lat), lambda i: (0, 0, 0)),
            pl.BlockSpec((tm, LANES), tab),
            pl.BlockSpec((tm, LANES), tab),
        ],
        out_specs=[
            pl.BlockSpec((tm, lat + C_ROPE), row),
            pl.BlockSpec((tm, dq), row),
            pl.BlockSpec((tm, n_heads * dq), row),
        ],
        out_shape=[
            jax.ShapeDtypeStruct((m, lat + C_ROPE), F32),
            jax.ShapeDtypeStruct((m, dq), BF16),
            jax.ShapeDtypeStruct((m, n_heads * dq), BF16),
        ],
        compiler_params=_params("parallel"),
        name="c_project",
    )(x, g.reshape(1, d), w["q"], w["kv"], w["pe"], q_norm.reshape(1, -1), kv_norm.reshape(1, -1),
      w["qn"], w["qr"], w["uk"], cos_t, sin_t)


def _c_sample_kernel(n_q, n_heads, lat, n_group, pt_ref, q_ref, kn_ref, *refs):
    page_refs = refs[:n_group]
    o_ref, m_scr, l_scr, acc_scr, kn_scr, kr_scr = refs[n_group:]
    p = pl.program_id(1)
    rows = n_q * n_heads
    slots = page_refs[0].shape[2]
    q = q_ref[0]

    @pl.when(p == 0)
    def _():
        kn_scr[...] = jnp.zeros_like(kn_scr)
        kr_scr[...] = jnp.zeros_like(kr_scr)
        kn_scr[0:n_q, :] = kn_ref[0].astype(F32)
        knew = kn_scr[...].astype(BF16)
        s = _dot_nt(q, knew)
        col = lax.broadcasted_iota(jnp.int32, (rows, LANES), 1)
        qi = lax.broadcasted_iota(jnp.int32, (rows, LANES), 0) // n_heads
        s = jnp.where(col <= qi, s, NEG)
        _init_softmax(s, m_scr, l_scr, acc_scr, lambda pb: _dot(pb, knew[:, :lat]))

    pages = [r[0] for r in page_refs]
    lat_t = jnp.concatenate([pg[:lat] for pg in pages], axis=1).astype(BF16)
    for gi in range(n_group):
        kr_scr[0:C_ROPE, gi * slots:(gi + 1) * slots] = pages[gi][lat:lat + C_ROPE]
    s = _dot(q[:, :lat], lat_t) + _dot(q[:, lat:], kr_scr[...].astype(BF16))
    _online_update(s, m_scr, l_scr, acc_scr, lambda pb: _dot_nt(pb, lat_t))

    @pl.when(p == pl.num_programs(1) - 1)
    def _():
        o_ref[0] = acc_scr[...] / jnp.tile(l_scr[...], (1, lat // LANES))


def _c_sample(q, kn, cache, page_table, dec_batch, n_q, n_heads, lat, n_group):
    pool, slots, width = cache.shape
    n_pages = page_table.shape[1]
    n_group = min(n_group, n_pages)
    assert n_pages % n_group == 0 and slots % LANES == 0 and lat % LANES == 0
    cache_t = jnp.transpose(cache, (0, 2, 1))
    rows = n_q * n_heads
    dq = lat + LANES
    in_specs = [
        pl.BlockSpec((1, rows, dq), lambda b, p, pt: (b, 0, 0)),
        pl.BlockSpec((1, n_q, dq), lambda b, p, pt: (b, 0, 0)),
    ]
    for gi in range(n_group):
        in_specs.append(pl.BlockSpec((1, width, slots), lambda b, p, pt, gi=gi: (pt[b, p * n_group + gi], 0, 0)))
    grid_spec = pltpu.PrefetchScalarGridSpec(
        num_scalar_prefetch=1,
        grid=(dec_batch, n_pages // n_group),
        in_specs=in_specs,
        out_specs=pl.BlockSpec((1, rows, lat), lambda b, p, pt: (b, 0, 0)),
        scratch_shapes=[
            pltpu.VMEM((rows, LANES), F32),
            pltpu.VMEM((rows, LANES), F32),
            pltpu.VMEM((rows, lat), F32),
            pltpu.VMEM((LANES, dq), F32),
            pltpu.VMEM((LANES, n_group * slots), F32),
        ],
    )
    return pl.pallas_call(
        functools.partial(_c_sample_kernel, n_q, n_heads, lat, n_group),
        grid_spec=grid_spec,
        out_shape=jax.ShapeDtypeStruct((dec_batch, rows, lat), F32),
        compiler_params=_params("parallel", "arbitrary"),
        name="c_sample",
    )(page_table, q, kn, *([cache_t] * n_group))


def _c_up_kernel(n_pairs, x_ref, ol_ref, wuv_ref, wo_ref, y_ref):
    ol = ol_ref[...]
    w = ol.shape[1] // n_pairs
    o = jnp.concatenate([_dot(ol[:, p * w:(p + 1) * w].astype(BF16), wuv_ref[p]) for p in range(n_pairs)], axis=1)
    y_ref[...] = x_ref[...] + _dot(o.astype(BF16), wo_ref[...])


def _c_up(x, o_lat, w_uv, w_out):
    m, d = x.shape
    k = o_lat.shape[1]
    tm = _tile(m, (256, 128, 64, 32, 16, 8))
    return pl.pallas_call(
        functools.partial(_c_up_kernel, w_uv.shape[0]),
        grid=(m // tm,),
        in_specs=[
            pl.BlockSpec((tm, d), lambda i: (i, 0)),
            pl.BlockSpec((tm, k), lambda i: (i, 0)),
            pl.BlockSpec(w_uv.shape, lambda i: (0, 0, 0)),
            pl.BlockSpec(w_out.shape, lambda i: (0, 0)),
        ],
        out_specs=pl.BlockSpec((tm, d), lambda i: (i, 0)),
        out_shape=jax.ShapeDtypeStruct((m, d), F32),
        compiler_params=_params("parallel"),
        name="c_up",
    )(x, o_lat, w_uv, w_out)


def _mixer_c(xp, xs, g, w_in, q_norm, kv_norm, w_qb, w_kvb, w_out, cache, page_table,
             batch, n, dec_batch, n_q, past_len):
    q_lora, lat = q_norm.shape[0], kv_norm.shape[0]
    n_heads = w_out.shape[0] // C_V
    dq = lat + LANES
    qb3 = w_qb.reshape(q_lora, n_heads, C_NOPE + C_ROPE)
    kvb3 = w_kvb.reshape(lat, n_heads, C_NOPE + C_V)
    uk_t = jnp.transpose(kvb3[..., :C_NOPE], (1, 2, 0))
    uk = jnp.zeros((n_heads // 2, LANES, 2 * lat), F32)
    uk = uk.at[:, :C_NOPE, :lat].set(uk_t[0::2]).at[:, C_NOPE:, lat:].set(uk_t[1::2])
    uv_h = jnp.transpose(kvb3[..., C_NOPE:], (1, 0, 2))
    uv = jnp.zeros((n_heads // 2, 2 * lat, LANES), F32)
    uv = uv.at[:, :lat, :C_V].set(uv_h[0::2]).at[:, lat:, C_V:].set(uv_h[1::2])
    w = {
        "q": w_in[:, :q_lora].astype(BF16),
        "kv": w_in[:, q_lora:q_lora + lat].astype(BF16),
        "pe": jnp.pad(w_in[:, q_lora + lat:], ((0, 0), (0, LANES - C_ROPE))).astype(BF16),
        "qn": qb3[..., :C_NOPE].reshape(q_lora, n_heads * C_NOPE).astype(BF16),
        "qr": jnp.pad(qb3[..., C_NOPE:], ((0, 0), (0, 0), (0, LANES - C_ROPE))).reshape(q_lora, n_heads * LANES).astype(BF16),
        "uk": uk.astype(BF16),
    }
    uv = uv.astype(BF16)
    w_o = w_out.astype(BF16)

    cos_p, sin_p = _rope_tables(jnp.arange(n), C_ROPE, 1, pad_to=LANES)
    st_p, kc_p, qa_p = _c_project(xp, g, w, q_norm, kv_norm, cos_p, sin_p, n_heads)
    o_lat_p = _flash(qa_p.reshape(batch, n, n_heads * dq), kc_p.reshape(batch, n, dq), None,
                     1, n_heads, dq, lat, 128, 512)
    xp = _c_up(xp, o_lat_p.reshape(batch * n, n_heads * lat), uv, w_o)

    cos_s, sin_s = _rope_tables(past_len + jnp.arange(n_q), C_ROPE, 1, pad_to=LANES)
    cos_s, sin_s = jnp.tile(cos_s, (dec_batch, 1)), jnp.tile(sin_s, (dec_batch, 1))
    st_s, kc_s, qa_s = _c_project(xs, g, w, q_norm, kv_norm, cos_s, sin_s, n_heads)
    o_lat_s = _c_sample(qa_s.reshape(dec_batch, n_q * n_heads, dq), kc_s.reshape(dec_batch, n_q, dq),
                        cache, page_table, dec_batch, n_q, n_heads, lat, C_PAGES_PER_STEP)
    o_lat_s = o_lat_s.reshape(dec_batch * n_q, n_heads * lat)
    xs = _c_up(xs, o_lat_s, uv, w_o)
    return xp, xs, st_p.reshape(batch, n, lat + C_ROPE), st_s.reshape(dec_batch, n_q, lat + C_ROPE)


def kernel(x_prompt, x_sample, cache_a_kv0, cache_a_kv1, cache_a_kv2, cache_b_kv, cache_b_logf, cache_c_ckv, page_table, p_prompt, p_sample, norm_g, final_g, ffn_w_in, ffn_w_out, ple_w_gate, ple_w_proj, a_w_in, a_w_out, b_w_in, b_f_bias, b_w_out, c_w_in, c_q_norm, c_kv_norm, c_w_qb, c_w_kvb, c_w_out):
    batch, n, d = x_prompt.shape
    dec_batch, n_q, _ = x_sample.shape
    depth = norm_g.shape[0]
    past_len = page_table.shape[1] * cache_b_kv.shape[2]
    xp = x_prompt.reshape(batch * n, d)
    xs = x_sample.reshape(dec_batch * n_q, d)
    n_a_layers = cache_a_kv0.shape[0]
    a_bufs = [jnp.transpose(buf, (0, 1, 3, 4, 5, 2)).reshape(buf.shape[0], buf.shape[1], 2, -1, buf.shape[2])
              for buf in (cache_a_kv0, cache_a_kv1, cache_a_kv2)]
    a_st_p, a_st_s = None, None
    b_kv_p, b_lf_p, b_kv_s, b_lf_s, c_p, c_s = [], [], [], [], [], []
    for i in range(depth):
        kind, j = i % N_MIXERS, i // N_MIXERS
        w1, w2 = ffn_w_in[i, 0].astype(BF16), ffn_w_out[i, 0].astype(BF16)
        xp = _ffn(xp, norm_g[i, 0], w1, w2)
        xs = _ffn(xs, norm_g[i, 0], w1, w2)
        if kind == 0:
            xp, xs, a_st_p, a_st_s = _mixer_a(xp, xs, norm_g[i, 1], a_w_in[j], a_w_out[j], a_bufs, a_st_p, a_st_s,
                                              j, n_a_layers, batch, n, dec_batch, n_q, past_len)
        elif kind == 1:
            xp, xs, kvp, lfp, kvs, lfs = _mixer_b(xp, xs, norm_g[i, 1], b_w_in[j], b_f_bias[j], b_w_out[j],
                                                  cache_b_kv[j], cache_b_logf[j], page_table,
                                                  batch, n, dec_batch, n_q)
            b_kv_p.append(kvp)
            b_lf_p.append(lfp)
            b_kv_s.append(kvs)
            b_lf_s.append(lfs)
        else:
            xp, xs, cp, cs = _mixer_c(xp, xs, norm_g[i, 1], c_w_in[j], c_q_norm[j], c_kv_norm[j], c_w_qb[j],
                                      c_w_kvb[j], c_w_out[j], cache_c_ckv[j], page_table,
                                      batch, n, dec_batch, n_q, past_len)
            c_p.append(cp)
            c_s.append(cs)
        w1, w2 = ffn_w_in[i, 1].astype(BF16), ffn_w_out[i, 1].astype(BF16)
        xp = _ffn(xp, norm_g[i, 2], w1, w2)
        xs = _ffn(xs, norm_g[i, 2], w1, w2)
        wg, wp = ple_w_gate[i].astype(BF16), ple_w_proj[i].astype(BF16)
        fg = final_g if i == depth - 1 else None
        xp = _ple(xp, p_prompt[i].reshape(batch * n, -1), norm_g[i, 3], wg, wp, fg)
        xs = _ple(xs, p_sample[i].reshape(dec_batch * n_q, -1), norm_g[i, 3], wg, wp, fg)
    def a_state(st):
        layers, rows, _, _, width = st.shape
        return jnp.transpose(st.reshape(layers, rows, 2, -1, HEAD_DIM, width), (0, 1, 5, 2, 3, 4))

    return (xp.reshape(batch, n, d), xs.reshape(dec_batch, n_q, d),
            a_state(a_st_p[0]), a_state(a_st_p[1]), a_state(a_st_p[2]),
            jnp.stack(b_kv_p), jnp.stack(b_lf_p), jnp.stack(c_p),
            a_state(a_st_s[0]), a_state(a_st_s[1]), a_state(a_st_s[2]),
            jnp.stack(b_kv_s), jnp.stack(b_lf_s), jnp.stack(c_s))
```

```python
import functools

import jax
import jax.numpy as jnp
from jax import lax
from jax.experimental import pallas as pl
from jax.experimental.pallas import tpu as pltpu

F32 = jnp.float32
BF16 = jnp.bfloat16

EPS = 1e-6
NEG = -1e30
ROPE_THETA = 10000.0
N_MIXERS = 3
A_GROUPS = ((128, 1), (512, 4), (2048, 16))
HEAD_DIM = 64
C_NOPE, C_ROPE, C_V = 64, 32, 64

LANES = 128
VMEM_LIMIT_BYTES = 56 * 1024 * 1024


def _params(*semantics):
    return pltpu.CompilerParams(dimension_semantics=semantics,
                                vmem_limit_bytes=VMEM_LIMIT_BYTES)


def _tile(m, candidates):
    for c in candidates:
        if m % c == 0:
            return c
    return m


def _rms(x, g):
    ms = jnp.mean(x * x, axis=-1, keepdims=True)
    return x * lax.rsqrt(ms + EPS) * g


def _dot(a, b):
    return jnp.dot(a, b, preferred_element_type=F32)


def _dot_nt(a, b):
    return lax.dot_general(a, b, (((1,), (1,)), ((), ())), preferred_element_type=F32)


def _swap_halves(y, half):
    width = y.shape[-1]
    lane = lax.broadcasted_iota(jnp.int32, y.shape, y.ndim - 1)
    first = (lane % (2 * half)) < half
    return jnp.where(first, pltpu.roll(y, width - half, y.ndim - 1), pltpu.roll(y, half, y.ndim - 1))


def _rope_tables(pos, dim, reps, pad_to=None):
    half = dim // 2
    inv = ROPE_THETA ** (-jnp.arange(half, dtype=F32) * 2.0 / dim)
    ang = pos.astype(F32)[:, None] * inv[None, :]
    cos, sin = jnp.cos(ang), jnp.sin(ang)
    c = jnp.concatenate([cos, cos], axis=-1)
    s = jnp.concatenate([-sin, sin], axis=-1)
    if pad_to is not None:
        z = jnp.zeros((pos.shape[0], pad_to - dim), F32)
        c = jnp.concatenate([c, z], axis=-1)
        s = jnp.concatenate([s, z], axis=-1)
    return jnp.tile(c, (1, reps)), jnp.tile(s, (1, reps))


def _ffn_kernel(x_ref, g_ref, wg_ref, wu_ref, wo_ref, o_ref, h_ref, acc_ref):
    j = pl.program_id(1)

    @pl.when(j == 0)
    def _():
        h_ref[...] = _rms(x_ref[...], g_ref[...]).astype(BF16)
        acc_ref[...] = jnp.zeros_like(acc_ref)

    h = h_ref[...]
    gate = _dot(h, wg_ref[...])
    up = _dot(h, wu_ref[...])
    act = (gate * jax.nn.sigmoid(gate) * up).astype(BF16)
    acc_ref[...] += _dot(act, wo_ref[...])

    @pl.when(j == pl.num_programs(1) - 1)
    def _():
        o_ref[...] = x_ref[...] + 0.5 * acc_ref[...]


def _ffn(x, g, w_in, w_out):
    m, d = x.shape
    f = w_out.shape[0]
    tm = _tile(m, (1024, 512, 256, 128, 64, 32, 16, 8))
    tf = _tile(f, (256, 128))
    nf = f // tf
    return pl.pallas_call(
        _ffn_kernel,
        grid=(m // tm, nf),
        in_specs=[
            pl.BlockSpec((tm, d), lambda i, j: (i, 0)),
            pl.BlockSpec((1, d), lambda i, j: (0, 0)),
            pl.BlockSpec((d, tf), lambda i, j: (0, j)),
            pl.BlockSpec((d, tf), lambda i, j: (0, j + nf)),
            pl.BlockSpec((tf, d), lambda i, j: (j, 0)),
        ],
        out_specs=pl.BlockSpec((tm, d), lambda i, j: (i, 0)),
        out_shape=jax.ShapeDtypeStruct((m, d), F32),
        scratch_shapes=[pltpu.VMEM((tm, d), BF16), pltpu.VMEM((tm, d), F32)],
        compiler_params=_params("parallel", "arbitrary"),
        name="ffn_half",
    )(x, g.reshape(1, d), w_in, w_in, w_out)


def _ple_kernel(x_ref, p_ref, g_ref, wg_ref, wp_ref, o_ref):
    x = x_ref[...]
    gate = jax.nn.sigmoid(_dot(_rms(x, g_ref[...]).astype(BF16), wg_ref[...]))
    o_ref[...] = x + gate * _dot(p_ref[...].astype(BF16), wp_ref[...])


def _ple_final_kernel(x_ref, p_ref, g_ref, wg_ref, wp_ref, fg_ref, o_ref):
    x = x_ref[...]
    gate = jax.nn.sigmoid(_dot(_rms(x, g_ref[...]).astype(BF16), wg_ref[...]))
    y = x + gate * _dot(p_ref[...].astype(BF16), wp_ref[...])
    o_ref[...] = _rms(y, fg_ref[...])


def _ple(x, p, g, w_gate, w_proj, final_g=None):
    m, d = x.shape
    pd = p.shape[1]
    tm = _tile(m, (512, 256, 128, 64, 32, 16, 8))
    row = lambda i: (i, 0)
    fix = lambda i: (0, 0)
    in_specs = [
        pl.BlockSpec((tm, d), row),
        pl.BlockSpec((tm, pd), row),
        pl.BlockSpec((1, d), fix),
        pl.BlockSpec((d, d), fix),
        pl.BlockSpec((pd, d), fix),
    ]
    args = [x, p, g.reshape(1, d), w_gate, w_proj]
    body = _ple_kernel
    if final_g is not None:
        in_specs.append(pl.BlockSpec((1, d), fix))
        args.append(final_g.reshape(1, d))
        body = _ple_final_kernel
    return pl.pallas_call(
        body,
        grid=(m // tm,),
        in_specs=in_specs,
        out_specs=pl.BlockSpec((tm, d), row),
        out_shape=jax.ShapeDtypeStruct((m, d), F32),
        compiler_params=_params("parallel"),
        name="ple",
    )(*args)


def _outproj_kernel(x_ref, o_ref_in, w_ref, y_ref):
    y_ref[...] = x_ref[...] + _dot(o_ref_in[...].astype(BF16), w_ref[...])


def _outproj(x, o, w):
    m, d = x.shape
    k = o.shape[1]
    tm = _tile(m, (512, 256, 128, 64, 32, 16, 8))
    return pl.pallas_call(
        _outproj_kernel,
        grid=(m // tm,),
        in_specs=[
            pl.BlockSpec((tm, d), lambda i: (i, 0)),
            pl.BlockSpec((tm, k), lambda i: (i, 0)),
            pl.BlockSpec((k, d), lambda i: (0, 0)),
        ],
        out_specs=pl.BlockSpec((tm, d), lambda i: (i, 0)),
        out_shape=jax.ShapeDtypeStruct((m, d), F32),
        compiler_params=_params("parallel"),
        name="outproj",
    )(x, o, w)


def _band_valid(tq, nk, offset, n_back):
    row = lax.broadcasted_iota(jnp.int32, (tq, nk), 0)
    col = lax.broadcasted_iota(jnp.int32, (tq, nk), 1)
    diff = row + offset - col
    return lax.bitcast_convert_type(diff, jnp.uint32) <= jnp.uint32(n_back)


def _a_combine_kernel(n_groups, *refs):
    x_ref = refs[0]
    o_refs = refs[1:1 + n_groups]
    l_refs = refs[1 + n_groups:1 + 2 * n_groups]
    w_ref, y_ref = refs[1 + 2 * n_groups:]
    lses = [r[...] for r in l_refs]
    mx = functools.reduce(jnp.maximum, lses)
    es = [jnp.exp(l - mx) for l in lses]
    den = functools.reduce(lambda a, b: a + b, es)
    num = functools.reduce(lambda a, b: a + b, [e * r[...].astype(F32) for e, r in zip(es, o_refs)])
    y_ref[...] = x_ref[...] + _dot((num / den).astype(BF16), w_ref[...])


def _a_combine(x, outs, lses, w_out):
    m, d = x.shape
    inner = w_out.shape[0]
    n_groups = len(outs)
    tm = _tile(m, (256, 128, 64, 32, 16, 8))
    row = lambda i: (i, 0)
    return pl.pallas_call(
        functools.partial(_a_combine_kernel, n_groups),
        grid=(m // tm,),
        in_specs=[pl.BlockSpec((tm, d), row)] + [pl.BlockSpec((tm, inner), row)] * (2 * n_groups)
        + [pl.BlockSpec((inner, d), lambda i: (0, 0))],
        out_specs=pl.BlockSpec((tm, d), row),
        out_shape=jax.ShapeDtypeStruct((m, d), F32),
        compiler_params=_params("parallel"),
        name="a_combine",
    )(x, *outs, *lses, w_out)


A_CHUNK = 512


def _rope_rows(x, cos, sin, n_heads):
    half = HEAD_DIM // 2
    parts = []
    for h in range(n_heads):
        parts.append(x[h * HEAD_DIM + half:(h + 1) * HEAD_DIM])
        parts.append(x[h * HEAD_DIM:h * HEAD_DIM + half])
    swapped = jnp.concatenate(parts, axis=0)
    return x * jnp.tile(cos, (n_heads, 1)) + swapped * jnp.tile(sin, (n_heads, 1))


def _rope_lanes(y, cos, sin):
    reps = y.shape[1] // LANES
    return y * jnp.tile(cos, (1, reps)) + _swap_halves(y, HEAD_DIM // 2) * jnp.tile(sin, (1, reps))


def _a_state_kernel(n_heads, x_ref, g_ref, wk_ref, wv_ref, cos_ref, sin_ref, *refs):
    o_ref = refs[-1]
    h = _rms(x_ref[...], g_ref[...]).astype(BF16)
    o_ref[0, 0, 0] = _rope_rows(_dot_nt(wk_ref[...], h), cos_ref[...], sin_ref[...], n_heads)
    o_ref[0, 0, 1] = _dot_nt(wv_ref[...], h)


def _a_state(x, g, wk_t, wv_t, cos_t, sin_t, prev, layer, n_layers, batch, n, window, n_heads):
    m, d = x.shape
    inner = n_heads * HEAD_DIM
    keep = min(window, n)
    tt = _tile(keep, (512, 256, 128))
    assert (n - keep) % tt == 0
    off = (n - keep) // tt
    fix = lambda b, t: (0, 0)
    in_specs = [
        pl.BlockSpec((tt, d), lambda b, t: (b * (n // tt) + off + t, 0)),
        pl.BlockSpec((1, d), fix),
        pl.BlockSpec((inner, d), fix),
        pl.BlockSpec((inner, d), fix),
        pl.BlockSpec((HEAD_DIM, tt), lambda b, t: (0, off + t)),
        pl.BlockSpec((HEAD_DIM, tt), lambda b, t: (0, off + t)),
    ]
    args = [x, g.reshape(1, d), wk_t, wv_t, cos_t, sin_t]
    aliases = {}
    if prev is not None:
        in_specs.append(pl.BlockSpec(memory_space=pl.ANY))
        args.append(prev)
        aliases = {len(args) - 1: 0}
    return pl.pallas_call(
        functools.partial(_a_state_kernel, n_heads),
        grid=(batch, keep // tt),
        in_specs=in_specs,
        out_specs=pl.BlockSpec((1, 1, 2, inner, tt), lambda b, t: (layer, b, 0, 0, t)),
        out_shape=jax.ShapeDtypeStruct((n_layers, batch, 2, inner, keep), F32),
        input_output_aliases=aliases,
        compiler_params=_params("parallel", "arbitrary"),
        name="a_state",
    )(*args)


def _a_prompt_kernel(scale, n_heads, n_back, halo, n_cls, x_ref, *refs):
    if halo:
        (xh_ref, g_ref, wq_ref, wk_ref, wv_ref, cq_ref, sq_ref, ck_ref, sk_ref, ckh_ref, skh_ref,
         o_ref, lse_ref, q_scr, kt_scr, vt_scr) = refs
    else:
        (g_ref, wq_ref, wk_ref, wv_ref, cq_ref, sq_ref, ck_ref, sk_ref,
         o_ref, lse_ref, q_scr, kt_scr, vt_scr) = refs
    c = pl.program_id(2)
    chunk = x_ref.shape[1]
    inner = n_heads * HEAD_DIM
    d = g_ref.shape[1]
    x = jnp.concatenate([x_ref[0, :, k * d:(k + 1) * d] for k in range(n_cls)], axis=0)
    cq = jnp.concatenate([cq_ref[:, k * LANES:(k + 1) * LANES] for k in range(n_cls)], axis=0)
    sq = jnp.concatenate([sq_ref[:, k * LANES:(k + 1) * LANES] for k in range(n_cls)], axis=0)
    ck = jnp.concatenate([ck_ref[k] for k in range(n_cls)], axis=1)
    sk = jnp.concatenate([sk_ref[k] for k in range(n_cls)], axis=1)
    h = _rms(x, g_ref[...]).astype(BF16)
    q_scr[...] = (_rope_lanes(_dot(h, wq_ref[...]), cq, sq) * scale).astype(BF16)
    kt_scr[:, halo:] = _rope_rows(_dot_nt(wk_ref[...], h), ck, sk, n_heads).astype(BF16)
    vt_scr[:, halo:] = _dot_nt(wv_ref[...], h).astype(BF16)
    if halo:
        hh = _rms(xh_ref[0], g_ref[...]).astype(BF16)
        kt_scr[:, :halo] = _rope_rows(_dot_nt(wk_ref[...], hh), ckh_ref[0], skh_ref[0], n_heads).astype(BF16)
        vt_scr[:, :halo] = _dot_nt(wv_ref[...], hh).astype(BF16)
    tq = min(LANES, chunk)
    low = lax.broadcasted_iota(jnp.int32, (tq, LANES), 1) < HEAD_DIM
    first_real = jnp.where(c > 0, 0, halo)

    def pair(hp, carry):
        r0 = pl.multiple_of(hp * LANES, LANES)
        for k in range(n_cls):
            for j in range(chunk // tq):
                q0 = j * tq
                if halo:
                    k0, nk, offset = q0, tq + halo, halo
                else:
                    k0 = max(0, q0 - n_back)
                    nk, offset = q0 + tq - k0, q0 - k0
                valid = _band_valid(tq, nk, offset, n_back)
                base = k * chunk
                qb = q_scr[base + q0:base + q0 + tq, pl.ds(r0, LANES)]
                kseg = kt_scr[pl.ds(r0, LANES), base + k0:base + k0 + nk]
                vseg = vt_scr[pl.ds(r0, LANES), base + k0:base + k0 + nk]
                zero = jnp.zeros_like(qb)
                q2 = jnp.concatenate([jnp.where(low, qb, zero), jnp.where(low, zero, qb)], axis=0)
                s = jnp.where(jnp.concatenate([valid, valid], axis=0), _dot(q2, kseg), NEG)
                if halo and j == 0:
                    s = jnp.where(lax.broadcasted_iota(jnp.int32, (2 * tq, nk), 1) >= first_real, s, NEG)
                mx = jnp.max(s, axis=-1, keepdims=True)
                p = jnp.exp(s - mx)
                den = jnp.sum(p, axis=-1, keepdims=True)
                out = _dot_nt(p.astype(BF16), vseg) / den
                lse = mx + jnp.log(den)
                o_ref[0, q0:q0 + tq, pl.ds(k * inner + r0, LANES)] = jnp.where(low, out[:tq], out[tq:]).astype(o_ref.dtype)
                lse_ref[0, q0:q0 + tq, pl.ds(k * inner + r0, LANES)] = jnp.where(low, lse[:tq], lse[tq:])
        return carry

    lax.fori_loop(0, inner // LANES, pair, 0)


def _a_prompt_group(x, g, wq, wk_t, wv_t, cos_l, sin_l, cos_r, sin_r, batch, n, gi, n_heads):
    window, dil = A_GROUPS[gi]
    n_back = window // dil
    d = x.shape[1]
    inner = n_heads * HEAD_DIM
    length = n // dil
    chunk = min(A_CHUNK, length)
    n_chunks = length // chunk
    halo = n_back if n_chunks > 1 else 0
    n_cls = max(1, min(dil, A_CHUNK // chunk))
    assert length % chunk == 0 and (halo == 0 or (chunk % halo == 0 and n_cls == 1)) and dil % n_cls == 0
    xv = x.reshape(batch, length, dil * d)
    clv = cos_l.reshape(length, dil * LANES)
    slv = sin_l.reshape(length, dil * LANES)
    fix = lambda b, r, c: (0, 0)
    prev = lambda c: jnp.maximum(c * (chunk // max(halo, 1)) - 1, 0)
    in_specs = [pl.BlockSpec((1, chunk, n_cls * d), lambda b, r, c: (b, c, r))]
    args = [xv]
    if halo:
        in_specs.append(pl.BlockSpec((1, halo, d), lambda b, r, c: (b, prev(c), r)))
        args.append(xv)
    in_specs += [
        pl.BlockSpec((1, d), fix),
        pl.BlockSpec((d, inner), fix),
        pl.BlockSpec((inner, d), fix),
        pl.BlockSpec((inner, d), fix),
        pl.BlockSpec((chunk, n_cls * LANES), lambda b, r, c: (c, r)),
        pl.BlockSpec((chunk, n_cls * LANES), lambda b, r, c: (c, r)),
        pl.BlockSpec((n_cls, HEAD_DIM, chunk), lambda b, r, c: (r, 0, c)),
        pl.BlockSpec((n_cls, HEAD_DIM, chunk), lambda b, r, c: (r, 0, c)),
    ]
    args += [g.reshape(1, d), wq, wk_t, wv_t, clv, slv, cos_r, sin_r]
    if halo:
        in_specs += [pl.BlockSpec((1, HEAD_DIM, halo), lambda b, r, c: (r, 0, prev(c)))] * 2
        args += [cos_r, sin_r]
    out_spec = pl.BlockSpec((1, chunk, n_cls * inner), lambda b, r, c: (b, c, r))
    shape = jax.ShapeDtypeStruct((batch, length, dil * inner), F32)
    shape_o = jax.ShapeDtypeStruct((batch, length, dil * inner), BF16)
    tokens = n_cls * chunk
    o, lse = pl.pallas_call(
        functools.partial(_a_prompt_kernel, HEAD_DIM ** -0.5, n_heads, n_back, halo, n_cls),
        grid=(batch, dil // n_cls, n_chunks),
        in_specs=in_specs,
        out_specs=[out_spec, out_spec],
        out_shape=[shape_o, shape],
        scratch_shapes=[pltpu.VMEM((tokens, inner), BF16), pltpu.VMEM((inner, tokens + halo), BF16),
                        pltpu.VMEM((inner, tokens + halo), BF16)],
        compiler_params=_params("parallel", "parallel", "arbitrary"),
        name=f"a_prompt_g{gi}",
    )(*args)
    return o.reshape(batch * n, inner), lse.reshape(batch * n, inner)


def _a_sample_project_kernel(scale, n_heads, x_ref, g_ref, wq_ref, wk_ref, wv_ref, cq_ref, sq_ref, ck_ref, sk_ref,
                             q_ref, kv_ref):
    h = _rms(x_ref[...], g_ref[...]).astype(BF16)
    q_ref[...] = _rope_lanes(_dot(h, wq_ref[0]), cq_ref[...], sq_ref[...]) * scale
    kv_ref[0, 0] = _rope_rows(_dot_nt(wk_ref[0], h), ck_ref[...], sk_ref[...], n_heads)
    kv_ref[0, 1] = _dot_nt(wv_ref[0], h)


def _a_sample_project(x, g, wq, wk_t, wv_t, cos_l, sin_l, cos_r, sin_r, n_heads):
    t, d = x.shape
    n_groups = wq.shape[0]
    inner = n_heads * HEAD_DIM
    fix = lambda gi: (0, 0)
    return pl.pallas_call(
        functools.partial(_a_sample_project_kernel, HEAD_DIM ** -0.5, n_heads),
        grid=(n_groups,),
        in_specs=[
            pl.BlockSpec((t, d), fix),
            pl.BlockSpec((1, d), fix),
            pl.BlockSpec((1, d, inner), lambda gi: (gi, 0, 0)),
            pl.BlockSpec((1, inner, d), lambda gi: (gi, 0, 0)),
            pl.BlockSpec((1, inner, d), lambda gi: (gi, 0, 0)),
            pl.BlockSpec((t, LANES), fix),
            pl.BlockSpec((t, LANES), fix),
            pl.BlockSpec((HEAD_DIM, t), fix),
            pl.BlockSpec((HEAD_DIM, t), fix),
        ],
        out_specs=[pl.BlockSpec((t, inner), lambda gi: (0, gi)),
                   pl.BlockSpec((1, 2, inner, t), lambda gi: (gi, 0, 0, 0))],
        out_shape=[jax.ShapeDtypeStruct((t, n_groups * inner), F32),
                   jax.ShapeDtypeStruct((n_groups, 2, inner, t), F32)],
        compiler_params=_params("arbitrary"),
        name="a_sample_project",
    )(x, g.reshape(1, d), wq, wk_t, wv_t, cos_l, sin_l, cos_r, sin_r)


A_SAMPLE_BLOCK_BYTES = 4 * 1024 * 1024
SUBLANES = 8


def _a_sample_kernel(n_q, dil, n_back, hpb, q_ref, new_ref, buf_ref, *refs):
    o_ref, lse_ref, st_ref = refs[-3:]
    b = pl.program_id(0)
    width = hpb * HEAD_DIM
    slots = max(hpb, SUBLANES)
    buf_len = buf_ref.shape[-1]
    n_tok = new_ref.shape[-1]
    buf = buf_ref[0, 0].reshape(2 * width, buf_len)
    new = new_ref[0].reshape(2 * width, n_tok)
    new_sel = pltpu.roll(new, n_tok - n_q - b * n_q, 1)
    lane_w = lax.broadcasted_iota(jnp.int32, (2 * width, buf_len), 1)
    state = jnp.where(lane_w >= buf_len - n_q, jnp.tile(new_sel, (1, buf_len // n_tok)),
                      pltpu.roll(buf, buf_len - n_q, 1))
    st_ref[0, 0] = state.reshape(2, width, buf_len)

    rows = n_q * slots
    own = (lax.broadcasted_iota(jnp.int32, (slots, width), 1) // HEAD_DIM
           == lax.broadcasted_iota(jnp.int32, (slots, width), 0))
    q = q_ref[0]
    qbd = jnp.concatenate(
        [jnp.where(own, jnp.broadcast_to(q[i:i + 1, :], (slots, width)), 0.0) for i in range(n_q)],
        axis=0).astype(BF16)

    def masked(s, diff):
        s = jnp.where((diff & (dil - 1)) == 0, s, NEG)
        return jnp.where(lax.bitcast_convert_type(diff, jnp.uint32) <= jnp.uint32(n_back * dil), s, NEG)

    qi = lax.broadcasted_iota(jnp.int32, (rows, buf_len), 0) // slots
    s_buf = masked(_dot(qbd, buf[:width].astype(BF16)),
                   buf_len + qi - lax.broadcasted_iota(jnp.int32, (rows, buf_len), 1))
    qi = lax.broadcasted_iota(jnp.int32, (rows, n_tok), 0) // slots
    lane = lax.broadcasted_iota(jnp.int32, (rows, n_tok), 1)
    s_new = masked(_dot(qbd, new_sel[:width].astype(BF16)),
                   jnp.where(lane >= n_tok - n_q, qi - (lane - (n_tok - n_q)), -1))
    mx = jnp.maximum(jnp.max(s_buf, axis=-1, keepdims=True), jnp.max(s_new, axis=-1, keepdims=True))
    p_buf = jnp.exp(s_buf - mx)
    p_new = jnp.exp(s_new - mx)
    den = jnp.sum(p_buf, axis=-1, keepdims=True) + jnp.sum(p_new, axis=-1, keepdims=True)
    out = (_dot_nt(p_buf.astype(BF16), buf[width:].astype(BF16))
           + _dot_nt(p_new.astype(BF16), new_sel[width:].astype(BF16))) / den
    lse = jnp.broadcast_to(mx + jnp.log(den), (rows, width))
    for i in range(n_q):
        sl = slice(i * slots, (i + 1) * slots)
        o_ref[0, i:i + 1, :] = jnp.sum(jnp.where(own, out[sl], 0.0), axis=0, keepdims=True)
        lse_ref[0, i:i + 1, :] = jnp.sum(jnp.where(own, lse[sl], 0.0), axis=0, keepdims=True)


def _a_sample_group(q, kv_new, buf_t, prev, layer, n_layers, dec_batch, n_q, gi, n_groups, n_heads):
    window, dil = A_GROUPS[gi]
    n_back = window // dil
    inner = n_heads * HEAD_DIM
    buf_len = buf_t.shape[-1]
    n_tok = kv_new.shape[-1]
    hpb = n_heads
    while hpb > 1 and 2 * hpb * HEAD_DIM * buf_len * 4 > A_SAMPLE_BLOCK_BYTES:
        hpb //= 2
    width = hpb * HEAD_DIM
    nhb = inner // width
    assert buf_len == window and buf_len % n_tok == 0 and n_tok == dec_batch * n_q and dil & (dil - 1) == 0
    in_specs = [
        pl.BlockSpec((1, n_q, width), lambda b, hb: (b, 0, gi * nhb + hb)),
        pl.BlockSpec((1, 2, width, n_tok), lambda b, hb: (gi, 0, hb, 0)),
        pl.BlockSpec((1, 1, 2, width, buf_len), lambda b, hb: (layer, b, 0, hb, 0)),
    ]
    args = [q, kv_new, buf_t]
    aliases = {}
    if prev is not None:
        in_specs.append(pl.BlockSpec(memory_space=pl.ANY))
        args.append(prev)
        aliases = {len(args) - 1: 2}
    out_spec = pl.BlockSpec((1, n_q, width), lambda b, hb: (b, 0, hb))
    shape = jax.ShapeDtypeStruct((dec_batch, n_q, inner), F32)
    o, lse, st = pl.pallas_call(
        functools.partial(_a_sample_kernel, n_q, dil, n_back, hpb),
        grid=(dec_batch, nhb),
        in_specs=in_specs,
        out_specs=[out_spec, out_spec,
                   pl.BlockSpec((1, 1, 2, width, buf_len), lambda b, hb: (layer, b, 0, hb, 0))],
        out_shape=[shape, shape, jax.ShapeDtypeStruct((n_layers, dec_batch, 2, inner, buf_len), F32)],
        input_output_aliases=aliases,
        compiler_params=_params("parallel", "arbitrary"),
        name=f"a_sample_g{gi}",
    )(*args)
    return o.reshape(dec_batch * n_q, inner), lse.reshape(dec_batch * n_q, inner), st


def _mixer_a(xp, xs, g, w_in, w_out, bufs_t, prev_p, prev_s, layer, n_layers, batch, n, dec_batch, n_q, past_len):
    n_groups = len(A_GROUPS)
    d = xp.shape[1]
    inner = w_out.shape[0]
    n_heads = inner // HEAD_DIM
    w4 = w_in.reshape(d, n_groups, 3, inner)
    wq = jnp.transpose(w4[:, :, 0], (1, 0, 2)).astype(BF16)
    wk_t = jnp.transpose(w4[:, :, 1], (1, 2, 0)).astype(BF16)
    wv_t = jnp.transpose(w4[:, :, 2], (1, 2, 0)).astype(BF16)
    w_o = w_out.astype(BF16)

    cos_p, sin_p = _rope_tables(jnp.arange(n), HEAD_DIM, 1)
    cos_pl, sin_pl = jnp.tile(cos_p, (1, 2)), jnp.tile(sin_p, (1, 2))
    outs_p, lses_p, st_p = [], [], []
    for gi, (window, dil) in enumerate(A_GROUPS):
        cos_r = jnp.transpose(cos_p.reshape(n // dil, dil, HEAD_DIM), (1, 2, 0))
        sin_r = jnp.transpose(sin_p.reshape(n // dil, dil, HEAD_DIM), (1, 2, 0))
        o, lse = _a_prompt_group(xp, g, wq[gi], wk_t[gi], wv_t[gi], cos_pl, sin_pl, cos_r, sin_r,
                                 batch, n, gi, n_heads)
        outs_p.append(o)
        lses_p.append(lse)
        st_p.append(_a_state(xp, g, wk_t[gi], wv_t[gi], cos_p.T, sin_p.T,
                             None if prev_p is None else prev_p[gi], layer, n_layers, batch, n, window, n_heads))

    cos_s, sin_s = _rope_tables(past_len + jnp.arange(n_q), HEAD_DIM, 1)
    cos_s, sin_s = jnp.tile(cos_s, (dec_batch, 1)), jnp.tile(sin_s, (dec_batch, 1))
    q_s, kv_new = _a_sample_project(xs, g, wq, wk_t, wv_t, jnp.tile(cos_s, (1, 2)), jnp.tile(sin_s, (1, 2)),
                                    cos_s.T, sin_s.T, n_heads)
    q_s = q_s.reshape(dec_batch, n_q, n_groups * inner)
    outs_s, lses_s, st_s = [], [], []
    for gi in range(n_groups):
        o, lse, st = _a_sample_group(q_s, kv_new, bufs_t[gi], None if prev_s is None else prev_s[gi],
                                     layer, n_layers, dec_batch, n_q, gi, n_groups, n_heads)
        outs_s.append(o)
        lses_s.append(lse)
        st_s.append(st)
    xp = _a_combine(xp, outs_p, lses_p, w_o)
    xs = _a_combine(xs, outs_s, lses_s, w_o)
    return xp, xs, st_p, st_s


def _rms_matmul_kernel(x_ref, g_ref, w_ref, o_ref, h_ref):
    @pl.when(pl.program_id(1) == 0)
    def _():
        h_ref[...] = _rms(x_ref[...], g_ref[...]).astype(BF16)

    o_ref[...] = _dot(h_ref[...], w_ref[...])


def _rms_matmul(x, g, w):
    m, d = x.shape
    n_out = w.shape[1]
    tm = _tile(m, (512, 256, 128, 64, 32, 16, 8))
    tn = _tile(n_out, (640, 512, 256, 128))
    return pl.pallas_call(
        _rms_matmul_kernel,
        grid=(m // tm, n_out // tn),
        in_specs=[
            pl.BlockSpec((tm, d), lambda i, j: (i, 0)),
            pl.BlockSpec((1, d), lambda i, j: (0, 0)),
            pl.BlockSpec((d, tn), lambda i, j: (0, j)),
        ],
        out_specs=pl.BlockSpec((tm, tn), lambda i, j: (i, j)),
        out_shape=jax.ShapeDtypeStruct((m, n_out), F32),
        scratch_shapes=[pltpu.VMEM((tm, d), BF16)],
        compiler_params=_params("parallel", "arbitrary"),
        name="rms_matmul",
    )(x, g.reshape(1, d), w)


def _flash_kernel(n_stack, dk, dv, v_from_k, tk, *refs):
    if v_from_k:
        q_ref, k_ref, o_ref, m_scr, l_scr, acc_scr = refs
        v_ref = k_ref
    else:
        q_ref, k_ref, v_ref, o_ref, m_scr, l_scr, acc_scr = refs
    tq = q_ref.shape[1]
    rows = n_stack * tq
    i = pl.program_id(2)
    qb = q_ref[0]
    q = qb if n_stack == 1 else jnp.concatenate([qb[:, s * dk:(s + 1) * dk] for s in range(n_stack)], axis=0)
    m_scr[...] = jnp.full(m_scr.shape, NEG, F32)
    l_scr[...] = jnp.zeros(l_scr.shape, F32)
    acc_scr[...] = jnp.zeros(acc_scr.shape, F32)

    def step(kb, masked):
        k0 = pl.multiple_of(kb * tk, tk)
        kblk = k_ref[0, pl.ds(k0, tk), :]
        vblk = kblk[:, :dv] if v_from_k else v_ref[0, pl.ds(k0, tk), :]
        s = _dot_nt(q, kblk)
        if masked:
            qpos = i * tq + lax.broadcasted_iota(jnp.int32, (rows, tk), 0) % tq
            col = lax.broadcasted_iota(jnp.int32, (rows, tk), 1)
            s = jnp.where(col + k0 <= qpos, s, NEG)
        m_old = m_scr[...]
        m_new = jnp.maximum(m_old, jnp.max(s, axis=-1, keepdims=True))
        alpha = jnp.exp(m_old - m_new)
        p = jnp.exp(s - jnp.tile(m_new, (1, tk // LANES)))
        l_scr[...] = alpha * l_scr[...] + jnp.sum(p, axis=-1, keepdims=True)
        acc_scr[...] = jnp.tile(alpha, (1, dv // LANES)) * acc_scr[...] + _dot(p.astype(BF16), vblk)
        m_scr[...] = m_new

    def full_body(kb, carry):
        step(kb, False)
        return carry

    def diag_body(kb, carry):
        step(kb, True)
        return carry

    n_full = lax.div(i * tq + 1, tk)
    n_total = lax.div((i + 1) * tq + tk - 1, tk)
    lax.fori_loop(0, n_full, full_body, 0)
    lax.fori_loop(n_full, n_total, diag_body, 0)
    out = acc_scr[...] / jnp.tile(l_scr[...], (1, dv // LANES))
    for s in range(n_stack):
        o_ref[0, :, s * dv:(s + 1) * dv] = out[s * tq:(s + 1) * tq, :].astype(o_ref.dtype)


def _flash(q, k, v, n_kv_heads, n_stack, dk, dv, tq, tk):
    batch, n, _ = q.shape
    tq = min(tq, n)
    tk = min(tk, n)
    assert n % tq == 0 and n % tk == 0 and tk % LANES == 0 and dv % LANES == 0
    v_from_k = v is None
    in_specs = [
        pl.BlockSpec((1, tq, n_stack * dk), lambda b, h, i: (b, i, h)),
        pl.BlockSpec((1, n, dk), lambda b, h, i: (b, 0, h)),
    ]
    args = [q, k]
    if not v_from_k:
        in_specs.append(pl.BlockSpec((1, n, dv), lambda b, h, i: (b, 0, h)))
        args.append(v)
    rows = n_stack * tq
    return pl.pallas_call(
        functools.partial(_flash_kernel, n_stack, dk, dv, v_from_k, tk),
        grid=(batch, n_kv_heads, n // tq),
        in_specs=in_specs,
        out_specs=pl.BlockSpec((1, tq, n_stack * dv), lambda b, h, i: (b, i, h)),
        out_shape=jax.ShapeDtypeStruct((batch, n, n_kv_heads * n_stack * dv), BF16),
        scratch_shapes=[pltpu.VMEM((rows, LANES), F32), pltpu.VMEM((rows, LANES), F32),
                        pltpu.VMEM((rows, dv), F32)],
        compiler_params=_params("parallel", "parallel", "arbitrary"),
        name="flash",
    )(*args)


B_PAGES_PER_STEP = 8
C_PAGES_PER_STEP = 16


def _log_sigmoid(z):
    return jnp.minimum(z, 0.0) - jnp.log1p(jnp.exp(-jnp.abs(z)))


def _split3(c):
    hi = c.astype(BF16)
    r = c - hi.astype(F32)
    mid = r.astype(BF16)
    lo = (r - mid.astype(F32)).astype(BF16)
    return hi, mid, lo


def _pad_heads(y):
    t, w = y.shape
    low = lax.broadcasted_iota(jnp.int32, (t, LANES), 1) < HEAD_DIM
    pieces = []
    for p in range(w // LANES):
        blk = y[:, p * LANES:(p + 1) * LANES]
        pieces.append(jnp.where(low, blk, 0.0))
        pieces.append(jnp.where(low, pltpu.roll(blk, HEAD_DIM, 1), 0.0))
    return jnp.concatenate(pieces, axis=1)


def _b_project_kernel(scale, n_heads, x_ref, g_ref, w_ref, wkv_t_ref, wf_ref, bias_ref, pq_ref, pk_ref, oq_ref, ok_ref,
                      kvt_ref, lf_ref, qa_ref, ka_ref, va_ref, carry_ref):
    @pl.when(pl.program_id(1) == 0)
    def _():
        carry_ref[...] = jnp.zeros_like(carry_ref)

    inner = n_heads * HEAD_DIM
    h = _rms(x_ref[...], g_ref[...]).astype(BF16)
    y = _dot(h, w_ref[...])
    z = _dot(h, wf_ref[...]) + bias_ref[...]
    tm = z.shape[0]
    lane = lax.broadcasted_iota(jnp.int32, (tm, LANES), 1)
    lf = jnp.where(lane < n_heads, _log_sigmoid(z), 0.0)
    lf_ref[...] = lf[:, :n_heads]
    tri = (lax.broadcasted_iota(jnp.int32, (tm, tm), 0) >= lax.broadcasted_iota(jnp.int32, (tm, tm), 1))
    tri = jnp.where(tri, 1.0, 0.0).astype(BF16)
    hi, mid, lo = _split3(lf)
    c = _dot(tri, hi) + _dot(tri, mid) + _dot(tri, lo) + carry_ref[...]
    carry_ref[...] = c[tm - 1:tm, :]
    c3 = jnp.concatenate(_split3(c), axis=1)
    q_aug = _dot(c3, pq_ref[...]) + oq_ref[...]
    k_aug = _dot(c3, pk_ref[...]) + ok_ref[...]
    q = y[:, :inner] * scale
    k = y[:, inner:2 * inner]
    v = y[:, 2 * inner:]
    kvt_ref[0, 0] = _dot_nt(wkv_t_ref[0], h)
    kvt_ref[0, 1] = _dot_nt(wkv_t_ref[1], h)
    qa_ref[...] = (_pad_heads(q) + q_aug).astype(BF16)
    ka_ref[...] = (_pad_heads(k) + k_aug).astype(BF16)
    va_ref[...] = _pad_heads(v).astype(BF16)


def _b_aug_constants(n_heads):
    pq = jnp.zeros((3 * LANES, n_heads * LANES), F32)
    pk = jnp.zeros((3 * LANES, n_heads * LANES), F32)
    oq = jnp.zeros((1, n_heads * LANES), F32)
    ok = jnp.zeros((1, n_heads * LANES), F32)
    heads = jnp.arange(n_heads)
    for piece in range(3):
        pq = pq.at[piece * LANES + heads, heads * LANES + HEAD_DIM + piece].set(1.0)
        pk = pk.at[piece * LANES + heads, heads * LANES + HEAD_DIM + 3 + piece].set(-1.0)
        ok = ok.at[0, heads * LANES + HEAD_DIM + piece].set(1.0)
        oq = oq.at[0, heads * LANES + HEAD_DIM + 3 + piece].set(1.0)
    return pq.astype(BF16), pk.astype(BF16), oq, ok


def _b_project(x, g, w_qkv, wkv_t, w_f, bias, batch, n, n_heads):
    m, d = x.shape
    inner = n_heads * HEAD_DIM
    tm = _tile(n, (256, 128, 64, 32, 16, 8))
    nt = n // tm
    pq, pk, oq, ok = _b_aug_constants(n_heads)
    row = lambda b, t: (b * nt + t, 0)
    fix = lambda b, t: (0, 0)
    wide = n_heads * LANES
    return pl.pallas_call(
        functools.partial(_b_project_kernel, HEAD_DIM ** -0.5, n_heads),
        grid=(batch, nt),
        in_specs=[
            pl.BlockSpec((tm, d), row),
            pl.BlockSpec((1, d), fix),
            pl.BlockSpec((d, 3 * inner), fix),
            pl.BlockSpec((2, inner, d), lambda b, t: (0, 0, 0)),
            pl.BlockSpec((d, LANES), fix),
            pl.BlockSpec((1, LANES), fix),
            pl.BlockSpec((3 * LANES, wide), fix),
            pl.BlockSpec((3 * LANES, wide), fix),
            pl.BlockSpec((1, wide), fix),
            pl.BlockSpec((1, wide), fix),
        ],
        out_specs=[
            pl.BlockSpec((1, 2, inner, tm), lambda b, t: (b, 0, 0, t)),
            pl.BlockSpec((tm, n_heads), row),
            pl.BlockSpec((tm, wide), row),
            pl.BlockSpec((tm, wide), row),
            pl.BlockSpec((tm, wide), row),
        ],
        out_shape=[
            jax.ShapeDtypeStruct((batch, 2, inner, n), F32),
            jax.ShapeDtypeStruct((m, n_heads), F32),
            jax.ShapeDtypeStruct((m, wide), BF16),
            jax.ShapeDtypeStruct((m, wide), BF16),
            jax.ShapeDtypeStruct((m, wide), BF16),
        ],
        scratch_shapes=[pltpu.VMEM((1, LANES), F32)],
        compiler_params=_params("parallel", "arbitrary"),
        name="b_project",
    )(x, g.reshape(1, d), w_qkv, wkv_t, w_f, bias, pq, pk, oq, ok)


def _online_update(s, m_scr, l_scr, acc_scr, pv):
    m_old = m_scr[...]
    m_new = jnp.maximum(m_old, jnp.max(s, axis=-1, keepdims=True))
    alpha = jnp.exp(m_old - m_new)
    p = jnp.exp(s - jnp.tile(m_new, (1, s.shape[1] // LANES)))
    l_scr[...] = alpha * l_scr[...] + jnp.sum(p, axis=-1, keepdims=True)
    acc_scr[...] = jnp.tile(alpha, (1, acc_scr.shape[1] // LANES)) * acc_scr[...] + pv(p.astype(BF16))
    m_scr[...] = m_new


def _init_softmax(s, m_scr, l_scr, acc_scr, pv):
    mx = jnp.max(s, axis=-1, keepdims=True)
    pr = jnp.exp(s - mx)
    m_scr[...] = jnp.broadcast_to(mx, m_scr.shape)
    l_scr[...] = jnp.broadcast_to(jnp.sum(pr, axis=-1, keepdims=True), l_scr.shape)
    acc_scr[...] = pv(pr.astype(BF16))


def _b_sample_kernel(scale, n_q, n_heads, n_group, pt_ref, y_ref, bias_ref, *refs):
    kt_refs = refs[:n_group]
    vt_refs = refs[n_group:2 * n_group]
    lft_refs = refs[2 * n_group:3 * n_group]
    (o_ref, lfo_ref, qbd_scr, at_scr, m_scr, l_scr, acc_scr, tail_scr, kn_scr, vn_scr) = refs[3 * n_group:]
    p = pl.program_id(1)
    inner = n_heads * HEAD_DIM
    rows = n_q * n_heads
    slots = kt_refs[0].shape[3]

    @pl.when(p == 0)
    def _():
        y = y_ref[0]
        q = y[:, :inner] * scale
        z = y[:, 3 * inner:] + bias_ref[...]
        lane = lax.broadcasted_iota(jnp.int32, (n_q, LANES), 1)
        lf = jnp.where(lane < n_heads, _log_sigmoid(z), 0.0)
        lfo_ref[0] = lf
        run = [lf[0:1]]
        for i in range(1, n_q):
            run.append(run[-1] + lf[i:i + 1])
        eye = (lax.broadcasted_iota(jnp.int32, (n_heads, LANES), 0)
               == lax.broadcasted_iota(jnp.int32, (n_heads, LANES), 1))
        cols = [jnp.sum(jnp.where(eye, jnp.broadcast_to(r, (n_heads, LANES)), 0.0), axis=1, keepdims=True)
                for r in run]
        a_col = jnp.concatenate(cols, axis=0)
        at_scr[...] = jnp.broadcast_to(a_col, at_scr.shape)
        lane_head = lax.broadcasted_iota(jnp.int32, (n_heads, inner), 1) // HEAD_DIM
        own = lane_head == lax.broadcasted_iota(jnp.int32, (n_heads, inner), 0)
        qbd = jnp.concatenate(
            [jnp.where(own, jnp.broadcast_to(q[i:i + 1, :], (n_heads, inner)), 0.0) for i in range(n_q)],
            axis=0).astype(BF16)
        qbd_scr[...] = qbd
        kn_scr[...] = jnp.zeros_like(kn_scr)
        vn_scr[...] = jnp.zeros_like(vn_scr)
        kn_scr[0:n_q, :] = y[:, inner:2 * inner]
        vn_scr[0:n_q, :] = y[:, 2 * inner:3 * inner]
        s = _dot_nt(qbd, kn_scr[...].astype(BF16))
        col = lax.broadcasted_iota(jnp.int32, (rows, LANES), 1)
        a_key = jnp.zeros((rows, LANES), F32)
        for i in range(n_q):
            a_key = jnp.where(col == i, jnp.concatenate([cols[i]] * n_q, axis=0), a_key)
        s = s + a_col - a_key
        qi = lax.broadcasted_iota(jnp.int32, (rows, LANES), 0) // n_heads
        s = jnp.where(col <= qi, s, NEG)
        vn = vn_scr[...].astype(BF16)
        _init_softmax(s, m_scr, l_scr, acc_scr, lambda pb: _dot(pb, vn))
        tail_scr[...] = jnp.zeros_like(tail_scr)

    lfts = [r[0] for r in lft_refs]
    later = (lax.broadcasted_iota(jnp.int32, (slots, slots), 0) > lax.broadcasted_iota(jnp.int32, (slots, slots), 1))
    later = jnp.where(later, 1.0, 0.0).astype(BF16)
    pieces = [piece for lft in lfts for piece in _split3(lft)]
    r3 = _dot(jnp.concatenate(pieces, axis=0), later)
    qbd = qbd_scr[...]
    tail = tail_scr[...]
    parts = []
    for gi in range(n_group):
        base = 3 * n_heads * gi
        r = (r3[base:base + n_heads] + r3[base + n_heads:base + 2 * n_heads]
             + r3[base + 2 * n_heads:base + 3 * n_heads] + tail)
        parts.append(_dot(qbd, kt_refs[gi][0, 0].astype(BF16)) + jnp.concatenate([r] * n_q, axis=0))
        tail = tail + jnp.sum(lfts[gi], axis=1, keepdims=True)
    tail_scr[...] = tail
    s = jnp.concatenate(parts, axis=1) + jnp.tile(at_scr[...], (1, n_group * slots // LANES))

    def pv(pb):
        out = _dot_nt(pb[:, :slots], vt_refs[0][0, 0].astype(BF16))
        for gi in range(1, n_group):
            out = out + _dot_nt(pb[:, gi * slots:(gi + 1) * slots], vt_refs[gi][0, 0].astype(BF16))
        return out

    _online_update(s, m_scr, l_scr, acc_scr, pv)

    @pl.when(p == pl.num_programs(1) - 1)
    def _():
        lane_head = lax.broadcasted_iota(jnp.int32, (n_heads, inner), 1) // HEAD_DIM
        own = lane_head == lax.broadcasted_iota(jnp.int32, (n_heads, inner), 0)
        out = acc_scr[...] / jnp.tile(l_scr[...], (1, inner // LANES))
        for i in range(n_q):
            o_ref[0, i:i + 1, :] = jnp.sum(jnp.where(own, out[i * n_heads:(i + 1) * n_heads], 0.0),
                                           axis=0, keepdims=True)


def _b_sample(y, bias, cache_kv, cache_logf, page_table, dec_batch, n_q, n_heads, n_group):
    inner = n_heads * HEAD_DIM
    pool, slots = cache_kv.shape[0], cache_kv.shape[1]
    n_pages = page_table.shape[1]
    n_group = min(n_group, n_pages)
    assert n_pages % n_group == 0 and slots % LANES == 0
    kvt = jnp.transpose(cache_kv, (0, 2, 3, 4, 1)).reshape(pool, 2, inner, slots)
    lft = jnp.transpose(cache_logf, (0, 2, 1))
    yv = y.reshape(dec_batch, n_q, 3 * inner + LANES)
    rows = n_q * n_heads

    def page(b, p, pt, gi):
        return pt[b, n_pages - 1 - (p * n_group + gi)]

    in_specs = [
        pl.BlockSpec((1, n_q, 3 * inner + LANES), lambda b, p, pt: (b, 0, 0)),
        pl.BlockSpec((1, LANES), lambda b, p, pt: (0, 0)),
    ]
    for kind in (0, 1):
        for gi in range(n_group):
            in_specs.append(pl.BlockSpec((1, 1, inner, slots),
                                         lambda b, p, pt, gi=gi, kind=kind: (page(b, p, pt, gi), kind, 0, 0)))
    for gi in range(n_group):
        in_specs.append(pl.BlockSpec((1, n_heads, slots), lambda b, p, pt, gi=gi: (page(b, p, pt, gi), 0, 0)))
    grid_spec = pltpu.PrefetchScalarGridSpec(
        num_scalar_prefetch=1,
        grid=(dec_batch, n_pages // n_group),
        in_specs=in_specs,
        out_specs=[
            pl.BlockSpec((1, n_q, inner), lambda b, p, pt: (b, 0, 0)),
            pl.BlockSpec((1, n_q, LANES), lambda b, p, pt: (b, 0, 0)),
        ],
        scratch_shapes=[
            pltpu.VMEM((rows, inner), BF16),
            pltpu.VMEM((rows, LANES), F32),
            pltpu.VMEM((rows, LANES), F32),
            pltpu.VMEM((rows, LANES), F32),
            pltpu.VMEM((rows, inner), F32),
            pltpu.VMEM((n_heads, 1), F32),
            pltpu.VMEM((LANES, inner), F32),
            pltpu.VMEM((LANES, inner), F32),
        ],
    )
    o, lf = pl.pallas_call(
        functools.partial(_b_sample_kernel, HEAD_DIM ** -0.5, n_q, n_heads, n_group),
        grid_spec=grid_spec,
        out_shape=[jax.ShapeDtypeStruct((dec_batch, n_q, inner), F32),
                   jax.ShapeDtypeStruct((dec_batch, n_q, LANES), F32)],
        compiler_params=_params("parallel", "arbitrary"),
        name="b_sample",
    )(page_table, yv, bias, *([kvt] * (2 * n_group)), *([lft] * n_group))
    return o.reshape(dec_batch * n_q, inner), lf[:, :, :n_heads]


def _mixer_b(xp, xs, g, w_in, f_bias, w_out, cache_kv, cache_logf, page_table, batch, n, dec_batch, n_q):
    inner = w_out.shape[0]
    n_heads = inner // HEAD_DIM
    d = xp.shape[1]
    w_qkv = w_in[:, :3 * inner].astype(BF16)
    w_f = jnp.pad(w_in[:, 3 * inner:], ((0, 0), (0, LANES - n_heads))).astype(BF16)
    bias = jnp.pad(f_bias.reshape(1, n_heads), ((0, 0), (0, LANES - n_heads)))
    w_out_pad = jnp.pad(w_out.reshape(n_heads, HEAD_DIM, d),
                        ((0, 0), (0, LANES - HEAD_DIM), (0, 0))).reshape(n_heads * LANES, d).astype(BF16)

    wkv_t = jnp.transpose(w_in[:, inner:3 * inner].reshape(d, 2, inner), (1, 2, 0)).astype(BF16)
    kvt_p, lf_p, qa, ka, va = _b_project(xp, g, w_qkv, wkv_t, w_f, bias, batch, n, n_heads)
    kv_p = jnp.transpose(kvt_p.reshape(batch, 2, n_heads, HEAD_DIM, n), (0, 4, 1, 2, 3))
    wide = n_heads * LANES
    o_p = _flash(qa.reshape(batch, n, wide), ka.reshape(batch, n, wide), va.reshape(batch, n, wide),
                 n_heads, 1, LANES, LANES, 512, 512)
    xp = _outproj(xp, o_p.reshape(batch * n, wide), w_out_pad)

    y_s = _rms_matmul(xs, g, jnp.concatenate([w_qkv, w_f], axis=1))
    o_s, lf_s = _b_sample(y_s, bias, cache_kv, cache_logf, page_table, dec_batch, n_q, n_heads, B_PAGES_PER_STEP)
    xs = _outproj(xs, o_s, w_out.astype(BF16))
    kv_s = y_s[:, inner:3 * inner].reshape(dec_batch, n_q, 2, n_heads, HEAD_DIM)
    return (xp, xs, kv_p, lf_p.reshape(batch, n, n_heads), kv_s, lf_s)


def _c_project_kernel(scale, n_heads, x_ref, g_ref, wq_ref, wkv_ref, wpe_ref, qn_ref, kvn_ref, wqn_ref, wqr_ref,
                      wuk_ref, cos_ref, sin_ref, st_ref, kc_ref, qa_ref):
    h = _rms(x_ref[...], g_ref[...]).astype(BF16)
    cq = _rms(_dot(h, wq_ref[...]), qn_ref[...]).astype(BF16)
    ckv = _rms(_dot(h, wkv_ref[...]), kvn_ref[...])
    cos, sin = cos_ref[...], sin_ref[...]
    ype = _dot(h, wpe_ref[...])
    kpe = ype * cos + _swap_halves(ype, C_ROPE // 2) * sin
    lat = ckv.shape[1]
    st_ref[:, :lat] = ckv
    st_ref[:, lat:lat + C_ROPE] = kpe[:, :C_ROPE]
    kc_ref[:, :lat] = ckv.astype(BF16)
    kc_ref[:, lat:] = kpe.astype(BF16)
    qn = _dot(cq, wqn_ref[...]).astype(BF16)
    qr = _dot(cq, wqr_ref[...])
    dq = lat + LANES
    for pair in range(n_heads // 2):
        ql = _dot(qn[:, pair * LANES:(pair + 1) * LANES], wuk_ref[pair])
        for j in range(2):
            head = 2 * pair + j
            r = qr[:, head * LANES:(head + 1) * LANES]
            r = r * cos + _swap_halves(r, C_ROPE // 2) * sin
            qa_ref[:, head * dq:head * dq + lat] = (ql[:, j * lat:(j + 1) * lat] * scale).astype(BF16)
            qa_ref[:, head * dq + lat:(head + 1) * dq] = (r * scale).astype(BF16)


def _c_project(x, g, w, q_norm, kv_norm, cos_t, sin_t, n_heads):
    m, d = x.shape
    lat = w["kv"].shape[1]
    q_lora = w["q"].shape[1]
    p = cos_t.shape[0]
    tm = _tile(p, (256, 128, 64, 32, 16, 8))
    npb = p // tm
    dq = lat + LANES
    row = lambda i: (i, 0)
    fix = lambda i: (0, 0)
    tab = lambda i: (i % npb, 0)
    return pl.pallas_call(
        functools.partial(_c_project_kernel, (C_NOPE + C_ROPE) ** -0.5, n_heads),
        grid=(m // tm,),
        in_specs=[
            pl.BlockSpec((tm, d), row),
            pl.BlockSpec((1, d), fix),
            pl.BlockSpec((d, q_lora), fix),
            pl.BlockSpec((d, lat), fix),
            pl.BlockSpec((d, LANES), fix),
            pl.BlockSpec((1, q_lora), fix),
            pl.BlockSpec((1, lat), fix),
            pl.BlockSpec((q_lora, n_heads * C_NOPE), fix),
            pl.BlockSpec((q_lora, n_heads * LANES), fix),
            pl.BlockSpec((n_heads // 2, LANES, 2 * lat), lambda i: (0, 0, 0)),
            pl.BlockSpec((tm, LANES), tab),
            pl.BlockSpec((tm, LANES), tab),
        ],
        out_specs=[
            pl.BlockSpec((tm, lat + C_ROPE), row),
            pl.BlockSpec((tm, dq), row),
            pl.BlockSpec((tm, n_heads * dq), row),
        ],
        out_shape=[
            jax.ShapeDtypeStruct((m, lat + C_ROPE), F32),
            jax.ShapeDtypeStruct((m, dq), BF16),
            jax.ShapeDtypeStruct((m, n_heads * dq), BF16),
        ],
        compiler_params=_params("parallel"),
        name="c_project",
    )(x, g.reshape(1, d), w["q"], w["kv"], w["pe"], q_norm.reshape(1, -1), kv_norm.reshape(1, -1),
      w["qn"], w["qr"], w["uk"], cos_t, sin_t)


def _c_sample_kernel(n_q, n_heads, lat, n_group, pt_ref, q_ref, kn_ref, *refs):
    page_refs = refs[:n_group]
    o_ref, m_scr, l_scr, acc_scr, kn_scr, kr_scr = refs[n_group:]
    p = pl.program_id(1)
    rows = n_q * n_heads
    slots = page_refs[0].shape[2]
    q = q_ref[0]

    @pl.when(p == 0)
    def _():
        kn_scr[...] = jnp.zeros_like(kn_scr)
        kr_scr[...] = jnp.zeros_like(kr_scr)
        kn_scr[0:n_q, :] = kn_ref[0].astype(F32)
        knew = kn_scr[...].astype(BF16)
        s = _dot_nt(q, knew)
        col = lax.broadcasted_iota(jnp.int32, (rows, LANES), 1)
        qi = lax.broadcasted_iota(jnp.int32, (rows, LANES), 0) // n_heads
        s = jnp.where(col <= qi, s, NEG)
        _init_softmax(s, m_scr, l_scr, acc_scr, lambda pb: _dot(pb, knew[:, :lat]))

    pages = [r[0] for r in page_refs]
    lat_t = jnp.concatenate([pg[:lat] for pg in pages], axis=1).astype(BF16)
    for gi in range(n_group):
        kr_scr[0:C_ROPE, gi * slots:(gi + 1) * slots] = pages[gi][lat:lat + C_ROPE]
    s = _dot(q[:, :lat], lat_t) + _dot(q[:, lat:], kr_scr[...].astype(BF16))
    _online_update(s, m_scr, l_scr, acc_scr, lambda pb: _dot_nt(pb, lat_t))

    @pl.when(p == pl.num_programs(1) - 1)
    def _():
        o_ref[0] = acc_scr[...] / jnp.tile(l_scr[...], (1, lat // LANES))


def _c_sample(q, kn, cache, page_table, dec_batch, n_q, n_heads, lat, n_group):
    pool, slots, width = cache.shape
    n_pages = page_table.shape[1]
    n_group = min(n_group, n_pages)
    assert n_pages % n_group == 0 and slots % LANES == 0 and lat % LANES == 0
    cache_t = jnp.transpose(cache, (0, 2, 1))
    rows = n_q * n_heads
    dq = lat + LANES
    in_specs = [
        pl.BlockSpec((1, rows, dq), lambda b, p, pt: (b, 0, 0)),
        pl.BlockSpec((1, n_q, dq), lambda b, p, pt: (b, 0, 0)),
    ]
    for gi in range(n_group):
        in_specs.append(pl.BlockSpec((1, width, slots), lambda b, p, pt, gi=gi: (pt[b, p * n_group + gi], 0, 0)))
    grid_spec = pltpu.PrefetchScalarGridSpec(
        num_scalar_prefetch=1,
        grid=(dec_batch, n_pages // n_group),
        in_specs=in_specs,
        out_specs=pl.BlockSpec((1, rows, lat), lambda b, p, pt: (b, 0, 0)),
        scratch_shapes=[
            pltpu.VMEM((rows, LANES), F32),
            pltpu.VMEM((rows, LANES), F32),
            pltpu.VMEM((rows, lat), F32),
            pltpu.VMEM((LANES, dq), F32),
            pltpu.VMEM((LANES, n_group * slots), F32),
        ],
    )
    return pl.pallas_call(
        functools.partial(_c_sample_kernel, n_q, n_heads, lat, n_group),
        grid_spec=grid_spec,
        out_shape=jax.ShapeDtypeStruct((dec_batch, rows, lat), F32),
        compiler_params=_params("parallel", "arbitrary"),
        name="c_sample",
    )(page_table, q, kn, *([cache_t] * n_group))


def _c_up_kernel(n_pairs, x_ref, ol_ref, wuv_ref, wo_ref, y_ref):
    ol = ol_ref[...]
    w = ol.shape[1] // n_pairs
    o = jnp.concatenate([_dot(ol[:, p * w:(p + 1) * w].astype(BF16), wuv_ref[p]) for p in range(n_pairs)], axis=1)
    y_ref[...] = x_ref[...] + _dot(o.astype(BF16), wo_ref[...])


def _c_up(x, o_lat, w_uv, w_out):
    m, d = x.shape
    k = o_lat.shape[1]
    tm = _tile(m, (256, 128, 64, 32, 16, 8))
    return pl.pallas_call(
        functools.partial(_c_up_kernel, w_uv.shape[0]),
        grid=(m // tm,),
        in_specs=[
            pl.BlockSpec((tm, d), lambda i: (i, 0)),
            pl.BlockSpec((tm, k), lambda i: (i, 0)),
            pl.BlockSpec(w_uv.shape, lambda i: (0, 0, 0)),
            pl.BlockSpec(w_out.shape, lambda i: (0, 0)),
        ],
        out_specs=pl.BlockSpec((tm, d), lambda i: (i, 0)),
        out_shape=jax.ShapeDtypeStruct((m, d), F32),
        compiler_params=_params("parallel"),
        name="c_up",
    )(x, o_lat, w_uv, w_out)


def _mixer_c(xp, xs, g, w_in, q_norm, kv_norm, w_qb, w_kvb, w_out, cache, page_table,
             batch, n, dec_batch, n_q, past_len):
    q_lora, lat = q_norm.shape[0], kv_norm.shape[0]
    n_heads = w_out.shape[0] // C_V
    dq = lat + LANES
    qb3 = w_qb.reshape(q_lora, n_heads, C_NOPE + C_ROPE)
    kvb3 = w_kvb.reshape(lat, n_heads, C_NOPE + C_V)
    uk_t = jnp.transpose(kvb3[..., :C_NOPE], (1, 2, 0))
    uk = jnp.zeros((n_heads // 2, LANES, 2 * lat), F32)
    uk = uk.at[:, :C_NOPE, :lat].set(uk_t[0::2]).at[:, C_NOPE:, lat:].set(uk_t[1::2])
    uv_h = jnp.transpose(kvb3[..., C_NOPE:], (1, 0, 2))
    uv = jnp.zeros((n_heads // 2, 2 * lat, LANES), F32)
    uv = uv.at[:, :lat, :C_V].set(uv_h[0::2]).at[:, lat:, C_V:].set(uv_h[1::2])
    w = {
        "q": w_in[:, :q_lora].astype(BF16),
        "kv": w_in[:, q_lora:q_lora + lat].astype(BF16),
        "pe": jnp.pad(w_in[:, q_lora + lat:], ((0, 0), (0, LANES - C_ROPE))).astype(BF16),
        "qn": qb3[..., :C_NOPE].reshape(q_lora, n_heads * C_NOPE).astype(BF16),
        "qr": jnp.pad(qb3[..., C_NOPE:], ((0, 0), (0, 0), (0, LANES - C_ROPE))).reshape(q_lora, n_heads * LANES).astype(BF16),
        "uk": uk.astype(BF16),
    }
    uv = uv.astype(BF16)
    w_o = w_out.astype(BF16)

    cos_p, sin_p = _rope_tables(jnp.arange(n), C_ROPE, 1, pad_to=LANES)
    st_p, kc_p, qa_p = _c_project(xp, g, w, q_norm, kv_norm, cos_p, sin_p, n_heads)
    o_lat_p = _flash(qa_p.reshape(batch, n, n_heads * dq), kc_p.reshape(batch, n, dq), None,
                     1, n_heads, dq, lat, 128, 512)
    xp = _c_up(xp, o_lat_p.reshape(batch * n, n_heads * lat), uv, w_o)

    cos_s, sin_s = _rope_tables(past_len + jnp.arange(n_q), C_ROPE, 1, pad_to=LANES)
    cos_s, sin_s = jnp.tile(cos_s, (dec_batch, 1)), jnp.tile(sin_s, (dec_batch, 1))
    st_s, kc_s, qa_s = _c_project(xs, g, w, q_norm, kv_norm, cos_s, sin_s, n_heads)
    o_lat_s = _c_sample(qa_s.reshape(dec_batch, n_q * n_heads, dq), kc_s.reshape(dec_batch, n_q, dq),
                        cache, page_table, dec_batch, n_q, n_heads, lat, C_PAGES_PER_STEP)
    o_lat_s = o_lat_s.reshape(dec_batch * n_q, n_heads * lat)
    xs = _c_up(xs, o_lat_s, uv, w_o)
    return xp, xs, st_p.reshape(batch, n, lat + C_ROPE), st_s.reshape(dec_batch, n_q, lat + C_ROPE)


def kernel(x_prompt, x_sample, cache_a_kv0, cache_a_kv1, cache_a_kv2, cache_b_kv, cache_b_logf, cache_c_ckv, page_table, p_prompt, p_sample, norm_g, final_g, ffn_w_in, ffn_w_out, ple_w_gate, ple_w_proj, a_w_in, a_w_out, b_w_in, b_f_bias, b_w_out, c_w_in, c_q_norm, c_kv_norm, c_w_qb, c_w_kvb, c_w_out):
    batch, n, d = x_prompt.shape
    dec_batch, n_q, _ = x_sample.shape
    depth = norm_g.shape[0]
    past_len = page_table.shape[1] * cache_b_kv.shape[2]
    xp = x_prompt.reshape(batch * n, d)
    xs = x_sample.reshape(dec_batch * n_q, d)
    n_a_layers = cache_a_kv0.shape[0]
    a_bufs = [jnp.transpose(buf, (0, 1, 3, 4, 5, 2)).reshape(buf.shape[0], buf.shape[1], 2, -1, buf.shape[2])
              for buf in (cache_a_kv0, cache_a_kv1, cache_a_kv2)]
    a_st_p, a_st_s = None, None
    b_kv_p, b_lf_p, b_kv_s, b_lf_s, c_p, c_s = [], [], [], [], [], []
    for i in range(depth):
        kind, j = i % N_MIXERS, i // N_MIXERS
        w1, w2 = ffn_w_in[i, 0].astype(BF16), ffn_w_out[i, 0].astype(BF16)
        xp = _ffn(xp, norm_g[i, 0], w1, w2)
        xs = _ffn(xs, norm_g[i, 0], w1, w2)
        if kind == 0:
            xp, xs, a_st_p, a_st_s = _mixer_a(xp, xs, norm_g[i, 1], a_w_in[j], a_w_out[j], a_bufs, a_st_p, a_st_s,
                                              j, n_a_layers, batch, n, dec_batch, n_q, past_len)
        elif kind == 1:
            xp, xs, kvp, lfp, kvs, lfs = _mixer_b(xp, xs, norm_g[i, 1], b_w_in[j], b_f_bias[j], b_w_out[j],
                                                  cache_b_kv[j], cache_b_logf[j], page_table,
                                                  batch, n, dec_batch, n_q)
            b_kv_p.append(kvp)
            b_lf_p.append(lfp)
            b_kv_s.append(kvs)
            b_lf_s.append(lfs)
        else:
            xp, xs, cp, cs = _mixer_c(xp, xs, norm_g[i, 1], c_w_in[j], c_q_norm[j], c_kv_norm[j], c_w_qb[j],
                                      c_w_kvb[j], c_w_out[j], cache_c_ckv[j], page_table,
                                      batch, n, dec_batch, n_q, past_len)
            c_p.append(cp)
            c_s.append(cs)
        w1, w2 = ffn_w_in[i, 1].astype(BF16), ffn_w_out[i, 1].astype(BF16)
        xp = _ffn(xp, norm_g[i, 2], w1, w2)
        xs = _ffn(xs, norm_g[i, 2], w1, w2)
        wg, wp = ple_w_gate[i].astype(BF16), ple_w_proj[i].astype(BF16)
        fg = final_g if i == depth - 1 else None
        xp = _ple(xp, p_prompt[i].reshape(batch * n, -1), norm_g[i, 3], wg, wp, fg)
        xs = _ple(xs, p_sample[i].reshape(dec_batch * n_q, -1), norm_g[i, 3], wg, wp, fg)
    def a_state(st):
        layers, rows, _, _, width = st.shape
        return jnp.transpose(st.reshape(layers, rows, 2, -1, HEAD_DIM, width), (0, 1, 5, 2, 3, 4))

    return (xp.reshape(batch, n, d), xs.reshape(dec_batch, n_q, d),
            a_state(a_st_p[0]), a_state(a_st_p[1]), a_state(a_st_p[2]),
            jnp.stack(b_kv_p), jnp.stack(b_lf_p), jnp.stack(c_p),
            a_state(a_st_s[0]), a_state(a_st_s[1]), a_state(a_st_s[2]),
            jnp.stack(b_kv_s), jnp.stack(b_lf_s), jnp.stack(c_s))
```
